```python
import math
import jax, jax.numpy as jnp
from jax import lax
import numpy as np

D_MODEL = 1024
BATCH = 32
SEQ = 256
DEPTH = 4
DEC_BATCH = 8
DEC_SEQ = 4096
PAST_LEN = 256

GRID_W = 64
D_MIX = D_MODEL
D_CONV = D_MIX // 2
D_RET = D_MIX - D_CONV
N_RET_HEADS = 4
HEAD_DK = D_RET // N_RET_HEADS
HEAD_DV = D_RET // N_RET_HEADS
CONV_K = 31
CONV_PAD = (CONV_K - 1) // 2
CHUNK = 128
D_FF = 2816
N_SUB = 3
D_IN = 2 * D_CONV + 4 * D_RET
ALPHA = (2.0 * DEPTH) ** 0.25
BETA = (8.0 * DEPTH) ** -0.25
ROPE_BASE = 10000.0
EPS = 1e-5

kernel_name = 'hybrid_conv_retention_flow_step'


def layer_norm(x, g=None, b=None):
    xf = x.astype(jnp.float32)
    mu = jnp.mean(xf, -1, keepdims=True)
    var = jnp.mean(jnp.square(xf - mu), -1, keepdims=True)
    y = (xf - mu) * lax.rsqrt(var + EPS)
    if g is not None:
        y = y * g.astype(jnp.float32) + b.astype(jnp.float32)
    return y.astype(x.dtype)


def swiglu(h, w1, w2):
    a, b = jnp.split(h @ w1, 2, axis=-1)
    return (jax.nn.silu(a) * b) @ w2


def depthwise_conv(u, w, b):
    y = lax.conv_general_dilated(u, w.astype(u.dtype)[:, None, :], window_strides=(1,),
                                 padding=[(CONV_PAD, CONV_PAD)],
                                 dimension_numbers=('NWC', 'WIO', 'NWC'),
                                 feature_group_count=u.shape[-1])
    return y + b.astype(u.dtype)


def grid_rope_angles(rows):
    t = jnp.arange(rows * GRID_W)
    r = (t // GRID_W).astype(jnp.float32)
    col = (t % GRID_W).astype(jnp.float32)
    nf = HEAD_DK // 4
    inv = ROPE_BASE ** (-jnp.arange(nf, dtype=jnp.float32) / nf)
    return r[:, None] * inv, col[:, None] * inv


def rotate(x, ang):
    x1, x2 = jnp.split(x, 2, axis=-1)
    cos, sin = jnp.cos(ang), jnp.sin(ang)
    return jnp.concatenate([x1 * cos - x2 * sin, x1 * sin + x2 * cos], -1)


def apply_grid_rope(x, rope):
    ang_row, ang_col = rope
    xr, xc = jnp.split(x, 2, axis=-1)
    return jnp.concatenate([rotate(xr, ang_row), rotate(xc, ang_col)], -1)


def retention_chunkwise(q, k, v, log_g, r0):
    b, h, l, dk = q.shape
    dv = v.shape[-1]
    n = l // CHUNK
    qc = q.reshape(b, h, n, CHUNK, dk)
    kc = k.reshape(b, h, n, CHUNK, dk)
    vc = v.reshape(b, h, n, CHUNK, dv)
    idx = jnp.arange(CHUNK, dtype=jnp.float32)
    dist = idx[:, None] - idx[None, :]
    intra_decay = jnp.where(dist >= 0, jnp.exp(log_g[:, None, None] * jnp.maximum(dist, 0.0)), 0.0)
    scores = jnp.einsum('bhncd,bhnsd->bhncs', qc, kc) * intra_decay[None, :, None]
    o_intra = jnp.einsum('bhncs,bhnsv->bhncv', scores, vc)
    k_decay = jnp.exp(log_g[:, None] * (CHUNK - 1.0 - idx))
    q_decay = jnp.exp(log_g[:, None] * (idx + 1.0))
    chunk_kv = jnp.einsum('bhncd,hc,bhncv->nbhdv', kc, k_decay, vc)
    chunk_decay = jnp.exp(log_g * CHUNK)[None, :, None, None]

    def step(r, kv):
        return chunk_decay * r + kv, r

    r_final, r_prev = lax.scan(step, r0, chunk_kv)
    o_inter = jnp.einsum('bhncd,hc,nbhdv->bhncv', qc, q_decay, r_prev)
    return (o_intra + o_inter).reshape(b, h, l, dv), r_final


def mixer(h, w_in, w_out, conv_w, conv_b, conv_ln_g, conv_ln_b, log_gamma, r0_f, r0_b, rope):
    bsz, l, _ = h.shape
    p = h @ w_in
    ca, cg, q, k, v, g = jnp.split(
        p, [D_CONV, 2 * D_CONV, 2 * D_CONV + D_RET, 2 * D_CONV + 2 * D_RET, 2 * D_CONV + 3 * D_RET], axis=-1)
    u = ca * jax.nn.sigmoid(cg)
    if rope is None:
        u = depthwise_conv(u, conv_w, conv_b)
    else:
        rows = l // GRID_W
        u = depthwise_conv(u.reshape(bsz * rows, GRID_W, D_CONV), conv_w, conv_b).reshape(bsz, l, D_CONV)
    u = jax.nn.silu(layer_norm(u, conv_ln_g, conv_ln_b))

    def to_heads(t):
        return t.reshape(bsz, l, N_RET_HEADS, -1).transpose(0, 2, 1, 3).astype(jnp.float32)

    qh, kh, vh = to_heads(q), to_heads(k), to_heads(v)
    if rope is not None:
        qh = apply_grid_rope(qh, rope)
        kh = apply_grid_rope(kh, rope)
    qh = qh * (HEAD_DK ** -0.5)
    o_f, r_f = retention_chunkwise(qh, kh, vh, log_gamma[0], r0_f.astype(jnp.float32))
    o_b, r_b = retention_chunkwise(jnp.flip(qh, 2), jnp.flip(kh, 2), jnp.flip(vh, 2),
                                   log_gamma[1], r0_b.astype(jnp.float32))
    o = layer_norm(o_f + jnp.flip(o_b, 2))
    o = o.transpose(0, 2, 1, 3).reshape(bsz, l, D_RET).astype(h.dtype) * jax.nn.silu(g)
    y = jnp.concatenate([u, o], axis=-1) @ w_out
    return y, r_f, r_b


def layer(x, cond, w_ada, b_ada, ln_g, ln_b, ffn_w1, ffn_w2, w_in, w_out, conv_w, conv_b,
          conv_ln_g, conv_ln_b, log_gamma, r0_f, r0_b, rope):
    mod = (jax.nn.silu(cond) @ w_ada + b_ada)[:, None, :]
    m = jnp.split(mod, 3 * N_SUB, axis=-1)

    def modulate(t, i):
        return t * (1.0 + m[3 * i + 1]) + m[3 * i]

    x = layer_norm(ALPHA * x + 0.5 * m[2] * swiglu(modulate(x, 0), ffn_w1[0], ffn_w2[0]), ln_g[0], ln_b[0])
    y, r_f, r_b = mixer(modulate(x, 1), w_in, w_out, conv_w, conv_b, conv_ln_g, conv_ln_b,
                        log_gamma, r0_f, r0_b, rope)
    x = layer_norm(ALPHA * x + m[5] * y, ln_g[1], ln_b[1])
    x = layer_norm(ALPHA * x + 0.5 * m[8] * swiglu(modulate(x, 2), ffn_w1[1], ffn_w2[1]), ln_g[2], ln_b[2])
    return x, r_f, r_b


def setup_inputs(seed: int = 0) -> dict:
    key = jax.random.key(seed)
    ks = jax.random.split(key, 24)
    nrm = jax.random.normal
    f32 = jnp.float32
    base_logit = jnp.log(2.0 ** (5.0 + jnp.arange(N_RET_HEADS, dtype=f32)) - 1.0)
    return {
        'x_prompt': nrm(ks[0], (BATCH, SEQ, D_MODEL), f32),
        'x_sample': nrm(ks[1], (DEC_BATCH, DEC_SEQ, D_MODEL), f32),
        'state_ret_fwd': nrm(ks[2], (DEC_BATCH, DEPTH, N_RET_HEADS, HEAD_DK, HEAD_DV), f32),
        'state_ret_bwd': nrm(ks[3], (DEC_BATCH, DEPTH, N_RET_HEADS, HEAD_DK, HEAD_DV), f32),
        'c': nrm(ks[4], (DEC_BATCH, D_MODEL), f32),
        'c_ctx': nrm(ks[5], (D_MODEL,), f32),
        'w_ada': nrm(ks[6], (DEPTH, D_MODEL, 3 * N_SUB * D_MODEL), f32) * D_MODEL ** -0.5,
        'b_ada': 0.01 * nrm(ks[7], (DEPTH, 3 * N_SUB * D_MODEL), f32),
        'ln_g': 1.0 + 0.01 * nrm(ks[8], (DEPTH, N_SUB, D_MODEL), f32),
        'ln_b': 0.01 * nrm(ks[9], (DEPTH, N_SUB, D_MODEL), f32),
        'ffn_w1': nrm(ks[10], (DEPTH, 2, D_MODEL, 2 * D_FF), f32) * D_MODEL ** -0.5,
        'ffn_w2': nrm(ks[11], (DEPTH, 2, D_FF, D_MODEL), f32) * (BETA * D_FF ** -0.5),
        'w_in': nrm(ks[12], (DEPTH, D_MODEL, D_IN), f32) * D_MODEL ** -0.5,
        'w_out': nrm(ks[13], (DEPTH, D_MIX, D_MODEL), f32) * (BETA * D_MIX ** -0.5),
        'conv_w': nrm(ks[14], (DEPTH, CONV_K, D_CONV), f32) * CONV_K ** -0.5,
        'conv_b': 0.01 * nrm(ks[15], (DEPTH, D_CONV), f32),
        'conv_ln_g': 1.0 + 0.01 * nrm(ks[16], (DEPTH, D_CONV), f32),
        'conv_ln_b': 0.01 * nrm(ks[17], (DEPTH, D_CONV), f32),
        'ret_decay_logit': base_logit + 0.1 * nrm(ks[18], (DEPTH, 2, N_RET_HEADS), f32),
    }


def reference(x_prompt, x_sample, state_ret_fwd, state_ret_bwd, c, c_ctx, w_ada, b_ada, ln_g, ln_b,
              ffn_w1, ffn_w2, w_in, w_out, conv_w, conv_b, conv_ln_g, conv_ln_b, ret_decay_logit):
    rope = grid_rope_angles(x_sample.shape[1] // GRID_W)
    zero_state = jnp.zeros((x_prompt.shape[0], N_RET_HEADS, HEAD_DK, HEAD_DV), jnp.float32)
    ctx = x_prompt
    lat = x_sample
    ctx_cond = c_ctx[None, :]
    new_fwd = []
    new_bwd = []
    for l in range(DEPTH):
        log_gamma = jax.nn.log_sigmoid(ret_decay_logit[l].astype(jnp.float32))
        ctx, r_f, r_b = layer(ctx, ctx_cond, w_ada[l], b_ada[l], ln_g[l], ln_b[l], ffn_w1[l], ffn_w2[l],
                              w_in[l], w_out[l], conv_w[l], conv_b[l], conv_ln_g[l], conv_ln_b[l],
                              log_gamma, zero_state, zero_state, None)
        new_fwd.append(r_f)
        new_bwd.append(r_b)
        lat, _, _ = layer(lat, c, w_ada[l], b_ada[l], ln_g[l], ln_b[l], ffn_w1[l], ffn_w2[l],
                          w_in[l], w_out[l], conv_w[l], conv_b[l], conv_ln_g[l], conv_ln_b[l],
                          log_gamma, state_ret_fwd[:, l], state_ret_bwd[:, l], rope)
    new_state_ret_fwd = jnp.stack(new_fwd, axis=1)
    new_state_ret_bwd = jnp.stack(new_bwd, axis=1)
    return (ctx, lat, new_state_ret_fwd, new_state_ret_bwd)
```

```python
import functools

import jax
import jax.numpy as jnp
from jax import lax
from jax.experimental import pallas as pl
from jax.experimental.pallas import tpu as pltpu

D_MODEL = 1024
D_CONV = 512
D_RET = 512
N_HEADS = 4
HEAD_D = 128
CONV_K = 31
CONV_PAD = 15
CHUNK = 128
GRID_W = 64
D_FF = 2816
N_SUB = 3
D_IN = 2 * D_CONV + 4 * D_RET
D_MID = D_CONV + 4 * D_RET
ROPE_BASE = 10000.0
EPS = 1e-5
COND_ROWS = 16
CONV_ROWS = 64
HALO = 16
FF_CHUNK = 256
VMEM_LIMIT = 56 * 1024 * 1024

F32 = jnp.float32
BF16 = jnp.bfloat16


def _dot(a, b):
    return jnp.dot(a, b, preferred_element_type=F32)


def _sigmoid(x):
    return 1.0 / (1.0 + jnp.exp(-x))


def _layer_norm(x, g=None, b=None):
    mu = jnp.mean(x, axis=-1, keepdims=True)
    xc = x - mu
    var = jnp.mean(xc * xc, axis=-1, keepdims=True)
    y = xc * lax.rsqrt(var + EPS)
    if g is not None:
        y = y * g + b
    return y


def _ffn(xm, w1_ref, w2_ref):
    acc = None
    for c in range(D_FF // FF_CHUNK):
        lo = c * FF_CHUNK
        a = _dot(xm, w1_ref[:, lo:lo + FF_CHUNK])
        b = _dot(xm, w1_ref[:, D_FF + lo:D_FF + lo + FF_CHUNK])
        h = (a * _sigmoid(a) * b).astype(BF16)
        y = _dot(h, w2_ref[lo:lo + FF_CHUNK, :])
        acc = y if acc is None else acc + y
    return acc


def _mod_piece(mod_ref, i):
    return mod_ref[:, i * D_MODEL:(i + 1) * D_MODEL]


def _decay_rows(lg, offset, sign):
    i = lax.broadcasted_iota(jnp.int32, (CHUNK, HEAD_D), 0).astype(F32)
    return jnp.exp(lg * (offset + sign * i))


def _mod_kernel(cond_ref, w_ref, b_ref, out_ref):
    cnd = cond_ref[...]
    s = (cnd * _sigmoid(cnd)).astype(BF16)
    out_ref[...] = _dot(s, w_ref[...].astype(BF16)) + b_ref[...]


def _modulation(cond_all, w_ada, b_ada):
    depth = w_ada.shape[0]
    n_out = w_ada.shape[2]
    tn = n_out // 4
    return pl.pallas_call(
        _mod_kernel,
        grid=(depth, n_out // tn),
        in_specs=[
            pl.BlockSpec((COND_ROWS, D_MODEL), lambda l, j: (0, 0)),
            pl.BlockSpec((None, D_MODEL, tn), lambda l, j: (l, 0, j)),
            pl.BlockSpec((None, 1, tn), lambda l, j: (l, 0, j)),
        ],
        out_specs=pl.BlockSpec((None, COND_ROWS, tn), lambda l, j: (l, 0, j)),
        out_shape=jax.ShapeDtypeStruct((depth, COND_ROWS, n_out), F32),
        compiler_params=pltpu.CompilerParams(
            dimension_semantics=("arbitrary", "arbitrary"), vmem_limit_bytes=VMEM_LIMIT),
        name="adaln_modulation",
    )(cond_all, w_ada, b_ada.reshape(depth, 1, n_out))


def _stage_a_kernel(layer, is_lat, tm, tiles_per_seq, chunks_per_seq, alpha, *refs):
    if is_lat:
        (lg_ref, x_ref, mod_ref, lng_ref, lnb_ref, w1_ref, w2_ref, win_ref, cos_ref, sin_ref, r0_ref,
         x1_ref, mid_ref, rb_ref, state_ref, kdec_ref) = refs
    else:
        (lg_ref, x_ref, mod_ref, lng_ref, lnb_ref, w1_ref, w2_ref, win_ref,
         x1_ref, mid_ref, rb_ref, rfin_ref, state_ref, kdec_ref) = refs
    step = pl.program_id(0)

    @pl.when(step == 0)
    def _():
        for h in range(N_HEADS):
            lg = lg_ref[layer * 2 * N_HEADS + N_HEADS + h]
            kdec_ref[h] = _decay_rows(lg, 0.0, 1.0)
            kdec_ref[N_HEADS + h] = _decay_rows(lg, float(CHUNK), 0.0)

    x = x_ref[...]
    xm = (x * (1.0 + _mod_piece(mod_ref, 1)) + _mod_piece(mod_ref, 0)).astype(BF16)
    y = _ffn(xm, w1_ref, w2_ref)
    x1 = _layer_norm(alpha * x + 0.5 * _mod_piece(mod_ref, 2) * y, lng_ref[0:1, :], lnb_ref[0:1, :])
    x1_ref[...] = x1

    hm = (x1 * (1.0 + _mod_piece(mod_ref, 4)) + _mod_piece(mod_ref, 3)).astype(BF16)
    p = _dot(hm, win_ref[...])
    ca = p[:, 0:D_CONV]
    cg = p[:, D_CONV:2 * D_CONV]
    mid_ref[:, 0:D_CONV] = (ca * _sigmoid(cg)).astype(BF16)
    o = 2 * D_CONV
    q = p[:, o:o + D_RET]
    k = p[:, o + D_RET:o + 2 * D_RET]
    v = p[:, o + 2 * D_RET:o + 3 * D_RET]
    g = p[:, o + 3 * D_RET:o + 4 * D_RET]
    if is_lat:
        lane = lax.broadcasted_iota(jnp.int32, (tm, D_RET), 1)
        first_half = (lane % (HEAD_D // 2)) < (HEAD_D // 4)
        cos = jnp.concatenate([cos_ref[...]] * N_HEADS, axis=1)
        sin = jnp.concatenate([sin_ref[...]] * N_HEADS, axis=1)

        def rope(t):
            swapped = jnp.where(first_half, pltpu.roll(t, D_RET - HEAD_D // 4, 1),
                                pltpu.roll(t, HEAD_D // 4, 1))
            return t * cos + swapped * sin

        q = rope(q)
        k = rope(k)
    q = q * (HEAD_D ** -0.5)
    mid_ref[:, D_CONV:D_CONV + D_RET] = q.astype(BF16)
    mid_ref[:, D_CONV + D_RET:D_CONV + 2 * D_RET] = k.astype(BF16)
    vb = v.astype(BF16)
    mid_ref[:, D_CONV + 2 * D_RET:D_CONV + 3 * D_RET] = vb
    mid_ref[:, D_CONV + 3 * D_RET:D_CONV + 4 * D_RET] = (g * _sigmoid(g)).astype(BF16)

    n_chunks = tm // CHUNK
    if is_lat:
        @pl.when(step % tiles_per_seq == 0)
        def _():
            state_ref[...] = r0_ref[...]
    for h in range(N_HEADS):
        cols = slice(h * HEAD_D, (h + 1) * HEAD_D)
        r = state_ref[h] if is_lat else None
        for ci in reversed(range(n_chunks)):
            rows = slice(ci * CHUNK, (ci + 1) * CHUNK)
            if not is_lat and (ci + 1) % chunks_per_seq == 0:
                r = jnp.zeros((HEAD_D, HEAD_D), F32)
            rb_ref[ci, h] = r
            kd = (k[rows, cols] * kdec_ref[h]).astype(BF16)
            kv = lax.dot_general(kd, vb[rows, cols], (((0,), (0,)), ((), ())),
                                 preferred_element_type=F32)
            r = kdec_ref[N_HEADS + h] * r + kv
            if not is_lat and ci % chunks_per_seq == 0:
                rfin_ref[ci // chunks_per_seq, h] = r
        if is_lat:
            state_ref[h] = r


def _const_spec(shape, index):
    return pl.BlockSpec(shape, lambda i: index, pipeline_mode=pl.Buffered(1))


def _stage_a(layer, is_lat, alpha, x2d, seq_len, tm, lg, mod, ln_g, ln_b, w1, w2, w_in, rope_tabs, r0):
    n_tok = x2d.shape[0]
    nt = n_tok // tm
    tiles_per_seq = max(seq_len // tm, 1)
    n_chunks = tm // CHUNK
    n_seq = n_tok // seq_len
    rev = lambda i: nt - 1 - i

    if is_lat:
        group = lambda i: 1 + rev(i) // tiles_per_seq
    else:
        group = lambda i: 0
    in_specs = [
        pl.BlockSpec(memory_space=pltpu.SMEM),
        pl.BlockSpec((tm, D_MODEL), lambda i: (rev(i), 0)),
        pl.BlockSpec((None, None, 1, N_SUB * 3 * D_MODEL), lambda i: (layer, group(i), 0, 0)),
        _const_spec((None, N_SUB, D_MODEL), (layer, 0, 0)),
        _const_spec((None, N_SUB, D_MODEL), (layer, 0, 0)),
        _const_spec((None, None, D_MODEL, 2 * D_FF), (layer, 0, 0, 0)),
        _const_spec((None, None, D_FF, D_MODEL), (layer, 0, 0, 0)),
        _const_spec((None, D_MODEL, D_IN), (layer, 0, 0)),
    ]
    args = [lg, x2d, mod, ln_g, ln_b, w1, w2, w_in]
    out_shape = [
        jax.ShapeDtypeStruct((n_tok, D_MODEL), F32),
        jax.ShapeDtypeStruct((n_tok, D_MID), BF16),
        jax.ShapeDtypeStruct((n_tok // CHUNK, N_HEADS, HEAD_D, HEAD_D), F32),
    ]
    out_specs = [
        pl.BlockSpec((tm, D_MODEL), lambda i: (rev(i), 0)),
        pl.BlockSpec((tm, D_MID), lambda i: (rev(i), 0)),
        pl.BlockSpec((n_chunks, N_HEADS, HEAD_D, HEAD_D), lambda i: (rev(i), 0, 0, 0)),
    ]
    if is_lat:
        cos_t, sin_t = rope_tabs
        in_specs += [
            pl.BlockSpec((tm, HEAD_D), lambda i: (rev(i) % tiles_per_seq, 0)),
            pl.BlockSpec((tm, HEAD_D), lambda i: (rev(i) % tiles_per_seq, 0)),
            pl.BlockSpec((None, None, N_HEADS, HEAD_D, HEAD_D),
                         lambda i: (rev(i) // tiles_per_seq, layer, 0, 0, 0)),
        ]
        args += [cos_t, sin_t, r0]
    else:
        seqs_per_tile = tm // seq_len
        out_shape.append(jax.ShapeDtypeStruct((n_seq, N_HEADS, HEAD_D, HEAD_D), F32))
        out_specs.append(pl.BlockSpec((seqs_per_tile, N_HEADS, HEAD_D, HEAD_D),
                                      lambda i: (rev(i), 0, 0, 0)))
    return pl.pallas_call(
        functools.partial(_stage_a_kernel, layer, is_lat, tm, tiles_per_seq,
                          min(seq_len, tm) // CHUNK, alpha),
        grid=(nt,),
        in_specs=in_specs,
        out_specs=out_specs,
        out_shape=out_shape,
        scratch_shapes=[pltpu.VMEM((N_HEADS, HEAD_D, HEAD_D), F32),
                        pltpu.VMEM((2 * N_HEADS, CHUNK, HEAD_D), F32)],
        compiler_params=pltpu.CompilerParams(
            dimension_semantics=("arbitrary",), vmem_limit_bytes=VMEM_LIMIT),
        name=f"stage_a_{'lat' if is_lat else 'ctx'}_{layer}",
    )(*args)


def _stage_b_kernel(layer, is_lat, tm, tiles_per_seq, chunks_per_seq, conv_dom, alpha, *refs):
    if is_lat:
        (lg_ref, x1_ref, mid_ref, rb_ref, mod_ref, lng_ref, lnb_ref, cw_ref, cb_ref, clg_ref, clb_ref,
         wout_ref, w1_ref, w2_ref, r0_ref,
         out_ref, state_ref, tab_ref, upad_ref, mix_ref) = refs
    else:
        (lg_ref, x1_ref, mid_ref, rb_ref, mod_ref, lng_ref, lnb_ref, cw_ref, cb_ref, clg_ref, clb_ref,
         wout_ref, w1_ref, w2_ref,
         out_ref, rfin_ref, state_ref, tab_ref, upad_ref, mix_ref) = refs
    step = pl.program_id(0)
    n_dom = tm // conv_dom
    dom_rows = conv_dom + 2 * HALO

    @pl.when(step == 0)
    def _():
        ii = lax.broadcasted_iota(jnp.int32, (CHUNK, CHUNK), 0)
        jj = lax.broadcasted_iota(jnp.int32, (CHUNK, CHUNK), 1)
        dist = (ii - jj).astype(F32)
        for h in range(N_HEADS):
            lgf = lg_ref[layer * 2 * N_HEADS + h]
            lgb = lg_ref[layer * 2 * N_HEADS + N_HEADS + h]
            fwd = jnp.where(dist >= 0, jnp.exp(lgf * jnp.maximum(dist, 0.0)), 0.0)
            bwd = jnp.where(dist <= 0, jnp.exp(lgb * jnp.maximum(-dist, 0.0)), 0.0)
            tab_ref[h, 0] = fwd + bwd
            tab_ref[h, 1] = _decay_rows(lgf, 1.0, 1.0)
            tab_ref[h, 2] = _decay_rows(lgb, float(CHUNK), -1.0)
            tab_ref[h, 3] = _decay_rows(lgf, CHUNK - 1.0, -1.0)
            tab_ref[h, 4] = _decay_rows(lgf, float(CHUNK), 0.0)
        zeros = jnp.zeros((HALO, D_CONV), F32)
        for j in range(n_dom):
            upad_ref[j * dom_rows:j * dom_rows + HALO, :] = zeros
            upad_ref[j * dom_rows + HALO + conv_dom:(j + 1) * dom_rows, :] = zeros

    for j in range(n_dom):
        upad_ref[j * dom_rows + HALO:j * dom_rows + HALO + conv_dom, :] = (
            mid_ref[j * conv_dom:(j + 1) * conv_dom, 0:D_CONV].astype(F32))
    for j in range(n_dom):
        for r0 in range(0, conv_dom, CONV_ROWS):
            base = j * dom_rows + HALO - CONV_PAD + r0
            acc = jnp.broadcast_to(cb_ref[...], (CONV_ROWS, D_CONV))
            for d in range(CONV_K):
                acc = acc + cw_ref[d:d + 1, :] * upad_ref[base + d:base + d + CONV_ROWS, :]
            uc = _layer_norm(acc, clg_ref[...], clb_ref[...])
            rows = slice(j * conv_dom + r0, j * conv_dom + r0 + CONV_ROWS)
            mix_ref[rows, 0:D_CONV] = (uc * _sigmoid(uc)).astype(BF16)

    n_chunks = tm // CHUNK
    if is_lat:
        @pl.when(step % tiles_per_seq == 0)
        def _():
            state_ref[...] = r0_ref[...]
    for h in range(N_HEADS):
        qc = slice(D_CONV + h * HEAD_D, D_CONV + (h + 1) * HEAD_D)
        kc = slice(D_CONV + D_RET + h * HEAD_D, D_CONV + D_RET + (h + 1) * HEAD_D)
        vc = slice(D_CONV + 2 * D_RET + h * HEAD_D, D_CONV + 2 * D_RET + (h + 1) * HEAD_D)
        gc = slice(D_CONV + 3 * D_RET + h * HEAD_D, D_CONV + 3 * D_RET + (h + 1) * HEAD_D)
        r = state_ref[h] if is_lat else None
        for ci in range(n_chunks):
            rows = slice(ci * CHUNK, (ci + 1) * CHUNK)
            if not is_lat and ci % chunks_per_seq == 0:
                r = jnp.zeros((HEAD_D, HEAD_D), F32)
            qh = mid_ref[rows, qc]
            kh = mid_ref[rows, kc]
            vh = mid_ref[rows, vc]
            s = lax.dot_general(qh, kh, (((1,), (1,)), ((), ())), preferred_element_type=F32)
            o = _dot((s * tab_ref[h, 0]).astype(BF16), vh)
            qf = qh.astype(F32)
            qd = jnp.concatenate([(qf * tab_ref[h, 1]).astype(BF16),
                                  (qf * tab_ref[h, 2]).astype(BF16)], axis=1)
            rcat = jnp.concatenate([r.astype(BF16), rb_ref[ci, h].astype(BF16)], axis=0)
            o = o + _dot(qd, rcat)
            kd = (kh.astype(F32) * tab_ref[h, 3]).astype(BF16)
            kv = lax.dot_general(kd, vh, (((0,), (0,)), ((), ())), preferred_element_type=F32)
            r = tab_ref[h, 4] * r + kv
            if not is_lat and (ci + 1) % chunks_per_seq == 0:
                rfin_ref[ci // chunks_per_seq, h] = r
            on = _layer_norm(o)
            mix_ref[rows, D_CONV + h * HEAD_D:D_CONV + (h + 1) * HEAD_D] = (
                on * mid_ref[rows, gc].astype(F32)).astype(BF16)
        if is_lat:
            state_ref[h] = r

    y = _dot(mix_ref[...], wout_ref[...])
    x2 = _layer_norm(alpha * x1_ref[...] + _mod_piece(mod_ref, 5) * y, lng_ref[1:2, :], lnb_ref[1:2, :])
    xm = (x2 * (1.0 + _mod_piece(mod_ref, 7)) + _mod_piece(mod_ref, 6)).astype(BF16)
    y2 = _ffn(xm, w1_ref, w2_ref)
    out_ref[...] = _layer_norm(alpha * x2 + 0.5 * _mod_piece(mod_ref, 8) * y2,
                               lng_ref[2:3, :], lnb_ref[2:3, :])


def _stage_b(layer, is_lat, alpha, x1, mid, rb, seq_len, tm, lg, mod, ln_g, ln_b, conv_w, conv_b, conv_ln_g,
             conv_ln_b, w_out, w1, w2, r0):
    n_tok = x1.shape[0]
    nt = n_tok // tm
    tiles_per_seq = max(seq_len // tm, 1)
    n_chunks = tm // CHUNK
    n_seq = n_tok // seq_len
    conv_dom = GRID_W if is_lat else seq_len
    n_dom = tm // conv_dom

    if is_lat:
        group = lambda i: 1 + i // tiles_per_seq
    else:
        group = lambda i: 0
    in_specs = [
        pl.BlockSpec(memory_space=pltpu.SMEM),
        pl.BlockSpec((tm, D_MODEL), lambda i: (i, 0)),
        pl.BlockSpec((tm, D_MID), lambda i: (i, 0)),
        pl.BlockSpec((n_chunks, N_HEADS, HEAD_D, HEAD_D), lambda i: (i, 0, 0, 0)),
        pl.BlockSpec((None, None, 1, N_SUB * 3 * D_MODEL), lambda i: (layer, group(i), 0, 0)),
        _const_spec((None, N_SUB, D_MODEL), (layer, 0, 0)),
        _const_spec((None, N_SUB, D_MODEL), (layer, 0, 0)),
        _const_spec((None, CONV_K, D_CONV), (layer, 0, 0)),
        _const_spec((None, 1, D_CONV), (layer, 0, 0)),
        _const_spec((None, 1, D_CONV), (layer, 0, 0)),
        _const_spec((None, 1, D_CONV), (layer, 0, 0)),
        _const_spec((None, D_MODEL, D_MODEL), (layer, 0, 0)),
        _const_spec((None, None, D_MODEL, 2 * D_FF), (layer, 1, 0, 0)),
        _const_spec((None, None, D_FF, D_MODEL), (layer, 1, 0, 0)),
    ]
    args = [lg, x1, mid, rb, mod, ln_g, ln_b, conv_w, conv_b, conv_ln_g, conv_ln_b, w_out, w1, w2]
    out_shape = [jax.ShapeDtypeStruct((n_tok, D_MODEL), F32)]
    out_specs = [pl.BlockSpec((tm, D_MODEL), lambda i: (i, 0))]
    if is_lat:
        in_specs.append(pl.BlockSpec((None, None, N_HEADS, HEAD_D, HEAD_D),
                                     lambda i: (i // tiles_per_seq, layer, 0, 0, 0)))
        args.append(r0)
    else:
        seqs_per_tile = tm // seq_len
        out_shape.append(jax.ShapeDtypeStruct((n_seq, N_HEADS, HEAD_D, HEAD_D), F32))
        out_specs.append(pl.BlockSpec((seqs_per_tile, N_HEADS, HEAD_D, HEAD_D), lambda i: (i, 0, 0, 0)))
    return pl.pallas_call(
        functools.partial(_stage_b_kernel, layer, is_lat, tm, tiles_per_seq,
                          min(seq_len, tm) // CHUNK, conv_dom, alpha),
        grid=(nt,),
        in_specs=in_specs,
        out_specs=out_specs,
        out_shape=out_shape,
        scratch_shapes=[pltpu.VMEM((N_HEADS, HEAD_D, HEAD_D), F32),
                        pltpu.VMEM((N_HEADS, 5, CHUNK, CHUNK), F32),
                        pltpu.VMEM((n_dom * (conv_dom + 2 * HALO), D_CONV), F32),
                        pltpu.VMEM((tm, D_MODEL), BF16)],
        compiler_params=pltpu.CompilerParams(
            dimension_semantics=("arbitrary",), vmem_limit_bytes=VMEM_LIMIT),
        name=f"stage_b_{'lat' if is_lat else 'ctx'}_{layer}",
    )(*args)


def _rope_tables(seq_len):
    t = jnp.arange(seq_len)
    r = (t // GRID_W).astype(F32)
    col = (t % GRID_W).astype(F32)
    nf = HEAD_D // 4
    inv = ROPE_BASE ** (-jnp.arange(nf, dtype=F32) / nf)
    ang_row = r[:, None] * inv
    ang_col = col[:, None] * inv
    cos = jnp.concatenate([jnp.cos(ang_row)] * 2 + [jnp.cos(ang_col)] * 2, axis=1)
    sin = jnp.concatenate([-jnp.sin(ang_row), jnp.sin(ang_row), -jnp.sin(ang_col), jnp.sin(ang_col)],
                          axis=1)
    return cos, sin


def kernel(x_prompt, x_sample, state_ret_fwd, state_ret_bwd, c, c_ctx, w_ada, b_ada, ln_g, ln_b, ffn_w1,
           ffn_w2, w_in, w_out, conv_w, conv_b, conv_ln_g, conv_ln_b, ret_decay_logit):
    batch, seq, _ = x_prompt.shape
    dec_batch, dec_seq, _ = x_sample.shape
    depth = w_ada.shape[0]
    alpha = (2.0 * depth) ** 0.25
    assert dec_batch + 1 <= COND_ROWS and seq % CHUNK == 0 and dec_seq % CHUNK == 0

    cond_all = jnp.zeros((COND_ROWS, D_MODEL), F32).at[0].set(c_ctx).at[1:1 + dec_batch].set(c)
    mod = _modulation(cond_all, w_ada, b_ada).reshape(depth, COND_ROWS, 1, N_SUB * 3 * D_MODEL)
    lg = jax.nn.log_sigmoid(ret_decay_logit.astype(F32)).reshape(-1)
    rope_tabs = _rope_tables(dec_seq)
    w1b = ffn_w1.astype(BF16)
    w2b = ffn_w2.astype(BF16)
    winb = w_in.astype(BF16)
    woutb = w_out.astype(BF16)
    conv_b3 = conv_b.reshape(depth, 1, D_CONV)
    clg3 = conv_ln_g.reshape(depth, 1, D_CONV)
    clb3 = conv_ln_b.reshape(depth, 1, D_CONV)

    tm_ctx = min(512, batch * seq)
    tm_lat = min(512, dec_seq)
    ctx = x_prompt.reshape(batch * seq, D_MODEL)
    lat = x_sample.reshape(dec_batch * dec_seq, D_MODEL)
    new_fwd = []
    new_bwd = []
    for l in range(depth):
        x1, mid, rb, r_b = _stage_a(l, False, alpha, ctx, seq, tm_ctx, lg, mod, ln_g, ln_b, w1b, w2b,
                                    winb, None, None)
        ctx, r_f = _stage_b(l, False, alpha, x1, mid, rb, seq, tm_ctx, lg, mod, ln_g, ln_b, conv_w,
                            conv_b3, clg3, clb3, woutb, w1b, w2b, None)
        new_fwd.append(r_f)
        new_bwd.append(r_b)
        x1, mid, rb = _stage_a(l, True, alpha, lat, dec_seq, tm_lat, lg, mod, ln_g, ln_b, w1b, w2b,
                               winb, rope_tabs, state_ret_bwd)
        (lat,) = _stage_b(l, True, alpha, x1, mid, rb, dec_seq, tm_lat, lg, mod, ln_g, ln_b, conv_w,
                          conv_b3, clg3, clb3, woutb, w1b, w2b, state_ret_fwd)
    return (ctx.reshape(batch, seq, D_MODEL), lat.reshape(dec_batch, dec_seq, D_MODEL),
            jnp.stack(new_fwd, axis=1), jnp.stack(new_bwd, axis=1))
```

```python
import functools

import jax
import jax.numpy as jnp
from jax import lax
from jax.experimental import pallas as pl
from jax.experimental.pallas import tpu as pltpu

D_MODEL = 1024
D_CONV = 512
D_RET = 512
N_HEADS = 4
HEAD_D = 128
CONV_K = 31
CONV_PAD = 15
CHUNK = 128
GRID_W = 64
D_FF = 2816
N_SUB = 3
D_IN = 2 * D_CONV + 4 * D_RET
D_MID = D_CONV + 4 * D_RET
ROPE_BASE = 10000.0
EPS = 1e-5
SUBLANES = 8
COND_ROWS = 16
CONV_ROWS = 32
HALO = 16
FF_CHUNK = 256
N_FF_CHUNKS = D_FF // FF_CHUNK
VMEM_LIMIT = 56 * 1024 * 1024

F32 = jnp.float32
BF16 = jnp.bfloat16


def _dot(a, b):
    return jnp.dot(a, b, preferred_element_type=F32)


def _sigmoid(x):
    return 1.0 / (1.0 + jnp.exp(-x))


def _layer_norm(x, g=None, b=None):
    mu = jnp.mean(x, axis=-1, keepdims=True)
    xc = x - mu
    var = jnp.mean(xc * xc, axis=-1, keepdims=True)
    y = xc * lax.rsqrt(var + EPS)
    if g is not None:
        y = y * g + b
    return y


def _ffn(xm_ref, w1_ref, w2_ref, h_ref, between=None):
    for c in range(N_FF_CHUNKS):
        lo = c * FF_CHUNK
        a = _dot(xm_ref[...], w1_ref[:, lo:lo + FF_CHUNK])
        b = _dot(xm_ref[...], w1_ref[:, D_FF + lo:D_FF + lo + FF_CHUNK])
        h_ref[:, lo:lo + FF_CHUNK] = (a * _sigmoid(a) * b).astype(BF16)
        if between is not None:
            between(c)
    return _dot(h_ref[...], w2_ref[...])


def _mod_piece(mod_ref, i):
    return mod_ref[:, i * D_MODEL:(i + 1) * D_MODEL]


def _decay_rows(lg, offset, sign):
    i = lax.broadcasted_iota(jnp.int32, (CHUNK, HEAD_D), 0).astype(F32)
    return jnp.exp(lg * (offset + sign * i))


def _mod_kernel(cond_ref, w_ref, b_ref, out_ref):
    cnd = cond_ref[...]
    s = (cnd * _sigmoid(cnd)).astype(BF16)
    out_ref[...] = _dot(s, w_ref[...].astype(BF16)) + b_ref[...]


def _modulation(cond_all, w_ada, b_ada):
    depth = w_ada.shape[0]
    n_out = w_ada.shape[2]
    tn = n_out // 4
    return pl.pallas_call(
        _mod_kernel,
        grid=(depth, n_out // tn),
        in_specs=[
            pl.BlockSpec((COND_ROWS, D_MODEL), lambda l, j: (0, 0)),
            pl.BlockSpec((None, D_MODEL, tn), lambda l, j: (l, 0, j)),
            pl.BlockSpec((None, 1, tn), lambda l, j: (l, 0, j)),
        ],
        out_specs=pl.BlockSpec((None, COND_ROWS, tn), lambda l, j: (l, 0, j)),
        out_shape=jax.ShapeDtypeStruct((depth, COND_ROWS, n_out), F32),
        compiler_params=pltpu.CompilerParams(
            dimension_semantics=("arbitrary", "arbitrary"), vmem_limit_bytes=VMEM_LIMIT),
        name="adaln_modulation",
    )(cond_all, w_ada, b_ada.reshape(depth, 1, n_out))


def _stage_a_kernel(layer, is_lat, tm, tiles_per_seq, chunks_per_seq, alpha, *refs):
    if is_lat:
        (lg_ref, x_ref, mod_ref, lng_ref, lnb_ref, w1_ref, w2_ref, win_ref, cos_ref, sin_ref, r0_ref,
         x1_ref, mid_ref, rb_ref, state_ref, kdec_ref, xm_ref, h_ref) = refs
    else:
        (lg_ref, x_ref, mod_ref, lng_ref, lnb_ref, w1_ref, w2_ref, win_ref,
         x1_ref, mid_ref, rb_ref, rfin_ref, state_ref, kdec_ref, xm_ref, h_ref) = refs
    step = pl.program_id(0)

    @pl.when(step == 0)
    def _():
        for h in range(N_HEADS):
            lg = lg_ref[layer * 2 * N_HEADS + N_HEADS + h]
            kdec_ref[h] = _decay_rows(lg, 0.0, 1.0)
            kdec_ref[N_HEADS + h] = _decay_rows(lg, float(CHUNK), 0.0)

    xm_ref[...] = (x_ref[...] * (1.0 + _mod_piece(mod_ref, 1)) + _mod_piece(mod_ref, 0)).astype(BF16)
    y = _ffn(xm_ref, w1_ref, w2_ref, h_ref)
    x1 = _layer_norm(alpha * x_ref[...] + 0.5 * _mod_piece(mod_ref, 2) * y,
                     lng_ref[0:1, :], lnb_ref[0:1, :])
    x1_ref[...] = x1

    xm_ref[...] = (x1 * (1.0 + _mod_piece(mod_ref, 4)) + _mod_piece(mod_ref, 3)).astype(BF16)
    p = _dot(xm_ref[...], win_ref[...])
    ca = p[:, 0:D_CONV]
    cg = p[:, D_CONV:2 * D_CONV]
    mid_ref[:, 0:D_CONV] = (ca * _sigmoid(cg)).astype(BF16)
    o = 2 * D_CONV
    q = p[:, o:o + D_RET]
    k = p[:, o + D_RET:o + 2 * D_RET]
    v = p[:, o + 2 * D_RET:o + 3 * D_RET]
    g = p[:, o + 3 * D_RET:o + 4 * D_RET]
    if is_lat:
        lane = lax.broadcasted_iota(jnp.int32, (tm, D_RET), 1)
        first_half = (lane % (HEAD_D // 2)) < (HEAD_D // 4)
        cos = jnp.concatenate([cos_ref[...]] * N_HEADS, axis=1)
        sin = jnp.concatenate([sin_ref[...]] * N_HEADS, axis=1)

        def rope(t):
            swapped = jnp.where(first_half, pltpu.roll(t, D_RET - HEAD_D // 4, 1),
                                pltpu.roll(t, HEAD_D // 4, 1))
            return t * cos + swapped * sin

        q = rope(q)
        k = rope(k)
    q = q * (HEAD_D ** -0.5)
    mid_ref[:, D_CONV:D_CONV + D_RET] = q.astype(BF16)
    mid_ref[:, D_CONV + D_RET:D_CONV + 2 * D_RET] = k.astype(BF16)
    vb = v.astype(BF16)
    mid_ref[:, D_CONV + 2 * D_RET:D_CONV + 3 * D_RET] = vb
    mid_ref[:, D_CONV + 3 * D_RET:D_CONV + 4 * D_RET] = (g * _sigmoid(g)).astype(BF16)

    n_chunks = tm // CHUNK
    if is_lat:
        @pl.when(step % tiles_per_seq == 0)
        def _():
            state_ref[...] = r0_ref[...]
    for h in range(N_HEADS):
        cols = slice(h * HEAD_D, (h + 1) * HEAD_D)
        r = state_ref[h] if is_lat else None
        for ci in reversed(range(n_chunks)):
            rows = slice(ci * CHUNK, (ci + 1) * CHUNK)
            if not is_lat and (ci + 1) % chunks_per_seq == 0:
                r = jnp.zeros((HEAD_D, HEAD_D), F32)
            rb_ref[ci, h] = r
            kd = (k[rows, cols] * kdec_ref[h]).astype(BF16)
            kv = lax.dot_general(kd, vb[rows, cols], (((0,), (0,)), ((), ())),
                                 preferred_element_type=F32)
            r = kdec_ref[N_HEADS + h] * r + kv
            if not is_lat and ci % chunks_per_seq == 0:
                rfin_ref[ci // chunks_per_seq, h] = r
        if is_lat:
            state_ref[h] = r


def _const_spec(shape, index):
    return pl.BlockSpec(shape, lambda i: index, pipeline_mode=pl.Buffered(1))


def _stage_a(layer, is_lat, alpha, x2d, seq_len, tm, lg, mod, ln_g, ln_b, w1, w2, w_in, rope_tabs, r0):
    n_tok = x2d.shape[0]
    nt = n_tok // tm
    tiles_per_seq = max(seq_len // tm, 1)
    n_chunks = tm // CHUNK
    n_seq = n_tok // seq_len
    rev = lambda i: nt - 1 - i

    if is_lat:
        group = lambda i: 1 + rev(i) // tiles_per_seq
    else:
        group = lambda i: 0
    in_specs = [
        pl.BlockSpec(memory_space=pltpu.SMEM),
        pl.BlockSpec((tm, D_MODEL), lambda i: (rev(i), 0)),
        pl.BlockSpec((None, None, 1, N_SUB * 3 * D_MODEL), lambda i: (layer, group(i), 0, 0)),
        _const_spec((None, N_SUB, D_MODEL), (layer, 0, 0)),
        _const_spec((None, N_SUB, D_MODEL), (layer, 0, 0)),
        _const_spec((None, None, D_MODEL, 2 * D_FF), (layer, 0, 0, 0)),
        _const_spec((None, None, D_FF, D_MODEL), (layer, 0, 0, 0)),
        _const_spec((None, D_MODEL, D_IN), (layer, 0, 0)),
    ]
    args = [lg, x2d, mod, ln_g, ln_b, w1, w2, w_in]
    out_shape = [
        jax.ShapeDtypeStruct((n_tok, D_MODEL), F32),
        jax.ShapeDtypeStruct((n_tok, D_MID), BF16),
        jax.ShapeDtypeStruct((n_tok // CHUNK, N_HEADS, HEAD_D, HEAD_D), F32),
    ]
    out_specs = [
        pl.BlockSpec((tm, D_MODEL), lambda i: (rev(i), 0)),
        pl.BlockSpec((tm, D_MID), lambda i: (rev(i), 0)),
        pl.BlockSpec((n_chunks, N_HEADS, HEAD_D, HEAD_D), lambda i: (rev(i), 0, 0, 0)),
    ]
    if is_lat:
        cos_t, sin_t = rope_tabs
        in_specs += [
            pl.BlockSpec((tm, HEAD_D), lambda i: (rev(i) % tiles_per_seq, 0)),
            pl.BlockSpec((tm, HEAD_D), lambda i: (rev(i) % tiles_per_seq, 0)),
            pl.BlockSpec((None, None, N_HEADS, HEAD_D, HEAD_D),
                         lambda i: (rev(i) // tiles_per_seq, layer, 0, 0, 0)),
        ]
        args += [cos_t, sin_t, r0]
    else:
        seqs_per_tile = tm // seq_len
        out_shape.append(jax.ShapeDtypeStruct((n_seq, N_HEADS, HEAD_D, HEAD_D), F32))
        out_specs.append(pl.BlockSpec((seqs_per_tile, N_HEADS, HEAD_D, HEAD_D),
                                      lambda i: (rev(i), 0, 0, 0)))
    return pl.pallas_call(
        functools.partial(_stage_a_kernel, layer, is_lat, tm, tiles_per_seq,
                          min(seq_len, tm) // CHUNK, alpha),
        grid=(nt,),
        in_specs=in_specs,
        out_specs=out_specs,
        out_shape=out_shape,
        scratch_shapes=[pltpu.VMEM((N_HEADS, HEAD_D, HEAD_D), F32),
                        pltpu.VMEM((2 * N_HEADS, CHUNK, HEAD_D), F32),
                        pltpu.VMEM((tm, D_MODEL), BF16),
                        pltpu.VMEM((tm, D_FF), BF16)],
        compiler_params=pltpu.CompilerParams(
            dimension_semantics=("arbitrary",), vmem_limit_bytes=VMEM_LIMIT),
        name=f"stage_a_{'lat' if is_lat else 'ctx'}_{layer}",
    )(*args)


def _stage_b_kernel(layer, is_lat, tm, nt, tiles_per_seq, chunks_per_seq, conv_dom, shift_slots, alpha,
                    *refs):
    if is_lat:
        (lg_ref, x1_ref, mid_ref, rb_ref, mod_ref, lng_ref, lnb_ref, cw_ref, cb_ref, clg_ref, clb_ref,
         wout_ref, w1_ref, w2_ref, r0_ref,
         out_ref, state_ref, tab_ref, upad_ref, shift_ref, mix_ref, x2_ref, xm_ref, h_ref) = refs
    else:
        (lg_ref, x1_ref, mid_ref, rb_ref, mod_ref, lng_ref, lnb_ref, cw_ref, cb_ref, clg_ref, clb_ref,
         wout_ref, w1_ref, w2_ref,
         out_ref, rfin_ref, state_ref, tab_ref, upad_ref, shift_ref, mix_ref, x2_ref, xm_ref,
         h_ref) = refs
    step = pl.program_id(0)
    n_dom = tm // conv_dom
    dom_rows = conv_dom + 2 * HALO
    copy_rows = conv_dom + (CONV_K // SUBLANES) * SUBLANES
    n_chunks = tm // CHUNK

    @pl.when(step == 0)
    def _():
        ii = lax.broadcasted_iota(jnp.int32, (CHUNK, CHUNK), 0)
        jj = lax.broadcasted_iota(jnp.int32, (CHUNK, CHUNK), 1)
        dist = (ii - jj).astype(F32)
        for h in range(N_HEADS):
            lgf = lg_ref[layer * 2 * N_HEADS + h]
            lgb = lg_ref[layer * 2 * N_HEADS + N_HEADS + h]
            fwd = jnp.where(dist >= 0, jnp.exp(lgf * jnp.maximum(dist, 0.0)), 0.0)
            bwd = jnp.where(dist <= 0, jnp.exp(lgb * jnp.maximum(-dist, 0.0)), 0.0)
            tab_ref[h, 0] = fwd + bwd
            tab_ref[h, 1] = _decay_rows(lgf, 1.0, 1.0)
            tab_ref[h, 2] = _decay_rows(lgb, float(CHUNK), -1.0)
            tab_ref[h, 3] = _decay_rows(lgf, CHUNK - 1.0, -1.0)
            tab_ref[h, 4] = _decay_rows(lgf, float(CHUNK), 0.0)
        zeros = jnp.zeros((HALO, D_CONV), F32)
        for j in range(n_dom):
            upad_ref[j * dom_rows:j * dom_rows + HALO, :] = zeros
            upad_ref[j * dom_rows + HALO + conv_dom:(j + 1) * dom_rows, :] = zeros
        mix_ref[...] = jnp.zeros((tm, D_MODEL), BF16)

    if is_lat:
        @pl.when(jnp.minimum(step, nt - 1) % tiles_per_seq == 0)
        def _():
            state_ref[...] = r0_ref[...]

    units = []

    def conv_fill(j):
        def run():
            base = j * dom_rows
            upad_ref[base + HALO:base + HALO + conv_dom, :] = (
                mid_ref[j * conv_dom:(j + 1) * conv_dom, 0:D_CONV].astype(F32))
            for b in range(1, SUBLANES):
                shift_ref[j % shift_slots, b - 1] = upad_ref[base + b:base + b + copy_rows, :]
        return run

    def conv_block(j, r0):
        def run():
            acc = jnp.broadcast_to(cb_ref[...], (CONV_ROWS, D_CONV))
            for d in range(CONV_K):
                off = HALO - CONV_PAD + d
                a, b = off // SUBLANES, off % SUBLANES
                if b == 0:
                    lo = j * dom_rows + a * SUBLANES + r0
                    tap = upad_ref[lo:lo + CONV_ROWS, :]
                else:
                    lo = a * SUBLANES + r0
                    tap = shift_ref[j % shift_slots, b - 1, lo:lo + CONV_ROWS, :]
                acc = acc + cw_ref[d:d + 1, :] * tap
            uc = _layer_norm(acc, clg_ref[...], clb_ref[...])
            rows = slice(j * conv_dom + r0, j * conv_dom + r0 + CONV_ROWS)
            mix_ref[rows, 0:D_CONV] = (uc * _sigmoid(uc)).astype(BF16)
        return run

    for j in range(n_dom):
        units.append(conv_fill(j))
        for r0 in range(0, conv_dom, CONV_ROWS):
            units.append(conv_block(j, r0))

    carried = {}

    def ret_unit(h, ci):
        qc = slice(D_CONV + h * HEAD_D, D_CONV + (h + 1) * HEAD_D)
        kc = slice(D_CONV + D_RET + h * HEAD_D, D_CONV + D_RET + (h + 1) * HEAD_D)
        vc = slice(D_CONV + 2 * D_RET + h * HEAD_D, D_CONV + 2 * D_RET + (h + 1) * HEAD_D)
        gc = slice(D_CONV + 3 * D_RET + h * HEAD_D, D_CONV + 3 * D_RET + (h + 1) * HEAD_D)

        def run():
            rows = slice(ci * CHUNK, (ci + 1) * CHUNK)
            if ci % chunks_per_seq == 0:
                r = state_ref[h] if is_lat else jnp.zeros((HEAD_D, HEAD_D), F32)
            else:
                r = carried[h]
            qh = mid_ref[rows, qc]
            kh = mid_ref[rows, kc]
            vh = mid_ref[rows, vc]
            s = lax.dot_general(qh, kh, (((1,), (1,)), ((), ())), preferred_element_type=F32)
            o = _dot((s * tab_ref[h, 0]).astype(BF16), vh)
            qf = qh.astype(F32)
            qd = jnp.concatenate([(qf * tab_ref[h, 1]).astype(BF16),
                                  (qf * tab_ref[h, 2]).astype(BF16)], axis=1)
            rcat = jnp.concatenate([r.astype(BF16), rb_ref[ci, h].astype(BF16)], axis=0)
            o = o + _dot(qd, rcat)
            kd = (kh.astype(F32) * tab_ref[h, 3]).astype(BF16)
            kv = lax.dot_general(kd, vh, (((0,), (0,)), ((), ())), preferred_element_type=F32)
            r = tab_ref[h, 4] * r + kv
            carried[h] = r
            if (ci + 1) % chunks_per_seq == 0:
                if is_lat:
                    state_ref[h] = r
                else:
                    rfin_ref[ci // chunks_per_seq, h] = r
            on = _layer_norm(o)
            mix_ref[rows, D_CONV + h * HEAD_D:D_CONV + (h + 1) * HEAD_D] = (
                on * mid_ref[rows, gc].astype(F32)).astype(BF16)
        return run

    for ci in range(n_chunks):
        for h in range(N_HEADS):
            units.append(ret_unit(h, ci))

    y = _dot(mix_ref[...], wout_ref[...])
    n_slots = N_FF_CHUNKS + 1
    per_slot = -(-len(units) // n_slots)

    def run_units(slot):
        for u in units[slot * per_slot:(slot + 1) * per_slot]:
            u()

    run_units(0)
    x2 = _layer_norm(alpha * x1_ref[...] + _mod_piece(mod_ref, 5) * y, lng_ref[1:2, :], lnb_ref[1:2, :])
    x2_ref[...] = x2
    xm_ref[...] = (x2 * (1.0 + _mod_piece(mod_ref, 7)) + _mod_piece(mod_ref, 6)).astype(BF16)
    y2 = _ffn(xm_ref, w1_ref, w2_ref, h_ref, between=lambda c: run_units(c + 1))
    out_ref[...] = _layer_norm(alpha * x2_ref[...] + 0.5 * _mod_piece(mod_ref, 8) * y2,
                               lng_ref[2:3, :], lnb_ref[2:3, :])


def _stage_b(layer, is_lat, alpha, x1, mid, rb, seq_len, tm, lg, mod, ln_g, ln_b, conv_w, conv_b, conv_ln_g,
             conv_ln_b, w_out, w1, w2, r0):
    n_tok = x1.shape[0]
    nt = n_tok // tm
    tiles_per_seq = max(seq_len // tm, 1)
    n_chunks = tm // CHUNK
    n_seq = n_tok // seq_len
    conv_dom = GRID_W if is_lat else seq_len
    n_dom = tm // conv_dom
    shift_slots = min(n_dom, 2) if is_lat else 1
    copy_rows = conv_dom + (CONV_K // SUBLANES) * SUBLANES

    ftile = lambda i: jnp.maximum(i - 1, 0)
    mtile = lambda i: jnp.minimum(i, nt - 1)
    if is_lat:
        group = lambda i: 1 + ftile(i) // tiles_per_seq
    else:
        group = lambda i: 0
    in_specs = [
        pl.BlockSpec(memory_space=pltpu.SMEM),
        pl.BlockSpec((tm, D_MODEL), lambda i: (ftile(i), 0)),
        pl.BlockSpec((tm, D_MID), lambda i: (mtile(i), 0)),
        pl.BlockSpec((n_chunks, N_HEADS, HEAD_D, HEAD_D), lambda i: (mtile(i), 0, 0, 0)),
        pl.BlockSpec((None, None, 1, N_SUB * 3 * D_MODEL), lambda i: (layer, group(i), 0, 0)),
        _const_spec((None, N_SUB, D_MODEL), (layer, 0, 0)),
        _const_spec((None, N_SUB, D_MODEL), (layer, 0, 0)),
        _const_spec((None, CONV_K, D_CONV), (layer, 0, 0)),
        _const_spec((None, 1, D_CONV), (layer, 0, 0)),
        _const_spec((None, 1, D_CONV), (layer, 0, 0)),
        _const_spec((None, 1, D_CONV), (layer, 0, 0)),
        _const_spec((None, D_MODEL, D_MODEL), (layer, 0, 0)),
        _const_spec((None, None, D_MODEL, 2 * D_FF), (layer, 1, 0, 0)),
        _const_spec((None, None, D_FF, D_MODEL), (layer, 1, 0, 0)),
    ]
    args = [lg, x1, mid, rb, mod, ln_g, ln_b, conv_w, conv_b, conv_ln_g, conv_ln_b, w_out, w1, w2]
    out_shape = [jax.ShapeDtypeStruct((n_tok, D_MODEL), F32)]
    out_specs = [pl.BlockSpec((tm, D_MODEL), lambda i: (ftile(i), 0))]
    if is_lat:
        in_specs.append(pl.BlockSpec((None, None, N_HEADS, HEAD_D, HEAD_D),
                                     lambda i: (mtile(i) // tiles_per_seq, layer, 0, 0, 0)))
        args.append(r0)
    else:
        seqs_per_tile = tm // seq_len
        out_shape.append(jax.ShapeDtypeStruct((n_seq, N_HEADS, HEAD_D, HEAD_D), F32))
        out_specs.append(pl.BlockSpec((seqs_per_tile, N_HEADS, HEAD_D, HEAD_D),
                                      lambda i: (mtile(i), 0, 0, 0)))
    return pl.pallas_call(
        functools.partial(_stage_b_kernel, layer, is_lat, tm, nt, tiles_per_seq,
                          min(seq_len, tm) // CHUNK, conv_dom, shift_slots, alpha),
        grid=(nt + 1,),
        in_specs=in_specs,
        out_specs=out_specs,
        out_shape=out_shape,
        scratch_shapes=[pltpu.VMEM((N_HEADS, HEAD_D, HEAD_D), F32),
                        pltpu.VMEM((N_HEADS, 5, CHUNK, CHUNK), F32),
                        pltpu.VMEM((n_dom * (conv_dom + 2 * HALO), D_CONV), F32),
                        pltpu.VMEM((shift_slots, SUBLANES - 1, copy_rows, D_CONV), F32),
                        pltpu.VMEM((tm, D_MODEL), BF16),
                        pltpu.VMEM((tm, D_MODEL), F32),
                        pltpu.VMEM((tm, D_MODEL), BF16),
                        pltpu.VMEM((tm, D_FF), BF16)],
        compiler_params=pltpu.CompilerParams(
            dimension_semantics=("arbitrary",), vmem_limit_bytes=VMEM_LIMIT),
        name=f"stage_b_{'lat' if is_lat else 'ctx'}_{layer}",
    )(*args)


def _rope_tables(seq_len):
    t = jnp.arange(seq_len)
    r = (t // GRID_W).astype(F32)
    col = (t % GRID_W).astype(F32)
    nf = HEAD_D // 4
    inv = ROPE_BASE ** (-jnp.arange(nf, dtype=F32) / nf)
    ang_row = r[:, None] * inv
    ang_col = col[:, None] * inv
    cos = jnp.concatenate([jnp.cos(ang_row)] * 2 + [jnp.cos(ang_col)] * 2, axis=1)
    sin = jnp.concatenate([-jnp.sin(ang_row), jnp.sin(ang_row), -jnp.sin(ang_col), jnp.sin(ang_col)],
                          axis=1)
    return cos, sin


def kernel(x_prompt, x_sample, state_ret_fwd, state_ret_bwd, c, c_ctx, w_ada, b_ada, ln_g, ln_b, ffn_w1,
           ffn_w2, w_in, w_out, conv_w, conv_b, conv_ln_g, conv_ln_b, ret_decay_logit):
    batch, seq, _ = x_prompt.shape
    dec_batch, dec_seq, _ = x_sample.shape
    depth = w_ada.shape[0]
    alpha = (2.0 * depth) ** 0.25
    assert dec_batch + 1 <= COND_ROWS and seq % CHUNK == 0 and dec_seq % CHUNK == 0

    cond_all = jnp.zeros((COND_ROWS, D_MODEL), F32).at[0].set(c_ctx).at[1:1 + dec_batch].set(c)
    mod = _modulation(cond_all, w_ada, b_ada).reshape(depth, COND_ROWS, 1, N_SUB * 3 * D_MODEL)
    lg = jax.nn.log_sigmoid(ret_decay_logit.astype(F32)).reshape(-1)
    rope_tabs = _rope_tables(dec_seq)
    w1b = ffn_w1.astype(BF16)
    w2b = ffn_w2.astype(BF16)
    winb = w_in.astype(BF16)
    woutb = w_out.astype(BF16)
    conv_b3 = conv_b.reshape(depth, 1, D_CONV)
    clg3 = conv_ln_g.reshape(depth, 1, D_CONV)
    clb3 = conv_ln_b.reshape(depth, 1, D_CONV)

    tm_ctx = min(512, batch * seq)
    tm_lat = min(512, dec_seq)
    ctx = x_prompt.reshape(batch * seq, D_MODEL)
    lat = x_sample.reshape(dec_batch * dec_seq, D_MODEL)
    new_fwd = []
    new_bwd = []
    for l in range(depth):
        x1, mid, rb, r_b = _stage_a(l, False, alpha, ctx, seq, tm_ctx, lg, mod, ln_g, ln_b, w1b, w2b,
                                    winb, None, None)
        ctx, r_f = _stage_b(l, False, alpha, x1, mid, rb, seq, tm_ctx, lg, mod, ln_g, ln_b, conv_w,
                            conv_b3, clg3, clb3, woutb, w1b, w2b, None)
        new_fwd.append(r_f)
        new_bwd.append(r_b)
        x1, mid, rb = _stage_a(l, True, alpha, lat, dec_seq, tm_lat, lg, mod, ln_g, ln_b, w1b, w2b,
                               winb, rope_tabs, state_ret_bwd)
        (lat,) = _stage_b(l, True, alpha, x1, mid, rb, dec_seq, tm_lat, lg, mod, ln_g, ln_b, conv_w,
                          conv_b3, clg3, clb3, woutb, w1b, w2b, state_ret_fwd)
    return (ctx.reshape(batch, seq, D_MODEL), lat.reshape(dec_batch, dec_seq, D_MODEL),
            jnp.stack(new_fwd, axis=1), jnp.stack(new_bwd, axis=1))
```

```python
import functools

import jax
import jax.numpy as jnp
from jax import lax
from jax.experimental import pallas as pl
from jax.experimental.pallas import tpu as pltpu

D_MODEL = 1024
D_CONV = 512
D_RET = 512
N_HEADS = 4
HEAD_D = 128
CONV_K = 31
CONV_PAD = 15
CHUNK = 128
GRID_W = 64
D_FF = 2816
N_SUB = 3
D_IN = 2 * D_CONV + 4 * D_RET
D_MID = D_CONV + 4 * D_RET
ROPE_BASE = 10000.0
EPS = 1e-5
SUBLANES = 8
TOKEN_ROWS = 16
COND_ROWS = 16
CONV_ROWS = 32
CONV_WIN = 64
COPY_ROWS = CONV_WIN + (CONV_K // SUBLANES) * SUBLANES
SHIFT_SLOTS = 2
HALO = 16
FF_CHUNK = 256
N_FF_CHUNKS = D_FF // FF_CHUNK
VMEM_LIMIT = 56 * 1024 * 1024

F32 = jnp.float32
BF16 = jnp.bfloat16


def _dot(a, b):
    return jnp.dot(a, b, preferred_element_type=F32)


def _sigmoid(x):
    return 1.0 / (1.0 + jnp.exp(-x))


def _layer_norm(x, g=None, b=None):
    mu = jnp.mean(x, axis=-1, keepdims=True)
    xc = x - mu
    var = jnp.mean(xc * xc, axis=-1, keepdims=True)
    y = xc * lax.rsqrt(var + EPS)
    if g is not None:
        y = y * g + b
    return y


def _ffn_hidden_chunk(c, xm_ref, w1_ref, h_ref, rows=slice(None)):
    lo = c * FF_CHUNK
    a = _dot(xm_ref[rows, :], w1_ref[:, lo:lo + FF_CHUNK])
    b = _dot(xm_ref[rows, :], w1_ref[:, D_FF + lo:D_FF + lo + FF_CHUNK])
    h_ref[:, lo:lo + FF_CHUNK] = (a * _sigmoid(a) * b).astype(BF16)


def _ffn(xm_ref, w1_ref, w2_ref, h_ref):
    for c in range(N_FF_CHUNKS):
        _ffn_hidden_chunk(c, xm_ref, w1_ref, h_ref)
    return _dot(h_ref[...], w2_ref[...])


def _split_units(costs, work):
    total_cost, total_work = sum(costs), sum(work)
    starts, k, spent, done = [0], 0, 0, 0
    for w in work[:-1]:
        done += w
        while k < len(costs) and (spent + costs[k] / 2) * total_work <= done * total_cost:
            spent += costs[k]
            k += 1
        starts.append(k)
    starts.append(len(costs))
    return starts


def _mod_piece(mod_ref, i):
    return mod_ref[:, i * D_MODEL:(i + 1) * D_MODEL]


def _decay_rows(lg, offset, sign):
    i = lax.broadcasted_iota(jnp.int32, (CHUNK, HEAD_D), 0).astype(F32)
    return jnp.exp(lg * (offset + sign * i))


def _mod_kernel(cond_ref, w_ref, b_ref, out_ref):
    cnd = cond_ref[...]
    s = (cnd * _sigmoid(cnd)).astype(BF16)
    out_ref[...] = _dot(s, w_ref[...].astype(BF16)) + b_ref[...]


def _modulation(cond_all, w_ada, b_ada):
    depth = w_ada.shape[0]
    n_out = w_ada.shape[2]
    tn = n_out // 4
    return pl.pallas_call(
        _mod_kernel,
        grid=(depth, n_out // tn),
        in_specs=[
            pl.BlockSpec((COND_ROWS, D_MODEL), lambda l, j: (0, 0)),
            pl.BlockSpec((None, D_MODEL, tn), lambda l, j: (l, 0, j)),
            pl.BlockSpec((None, 1, tn), lambda l, j: (l, 0, j)),
        ],
        out_specs=pl.BlockSpec((None, COND_ROWS, tn), lambda l, j: (l, 0, j)),
        out_shape=jax.ShapeDtypeStruct((depth, COND_ROWS, n_out), F32),
        compiler_params=pltpu.CompilerParams(
            dimension_semantics=("arbitrary", "arbitrary"), vmem_limit_bytes=VMEM_LIMIT),
        name="adaln_modulation",
    )(cond_all, w_ada, b_ada.reshape(depth, 1, n_out))


def _stage_a_kernel(layer, is_lat, tm, tiles_per_seq, chunks_per_seq, alpha, *refs):
    if is_lat:
        (lg_ref, x_ref, mod_ref, lng_ref, lnb_ref, w1_ref, w2_ref, win_ref, cos_ref, sin_ref, r0_ref,
         x1_ref, mid_ref, rb_ref, state_ref, kdec_ref, xm_ref, h_ref) = refs
    else:
        (lg_ref, x_ref, mod_ref, lng_ref, lnb_ref, w1_ref, w2_ref, win_ref,
         x1_ref, mid_ref, rb_ref, rfin_ref, state_ref, kdec_ref, xm_ref, h_ref) = refs
    step = pl.program_id(0)

    @pl.when(step == 0)
    def _():
        for h in range(N_HEADS):
            lg = lg_ref[layer * 2 * N_HEADS + N_HEADS + h]
            kdec_ref[h] = _decay_rows(lg, 0.0, 1.0)
            kdec_ref[N_HEADS + h] = _decay_rows(lg, float(CHUNK), 0.0)

    xm_ref[...] = (x_ref[...] * (1.0 + _mod_piece(mod_ref, 1)) + _mod_piece(mod_ref, 0)).astype(BF16)
    y = _ffn(xm_ref, w1_ref, w2_ref, h_ref)
    x1 = _layer_norm(alpha * x_ref[...] + 0.5 * _mod_piece(mod_ref, 2) * y,
                     lng_ref[0:1, :], lnb_ref[0:1, :])
    x1_ref[...] = x1

    xm_ref[...] = (x1 * (1.0 + _mod_piece(mod_ref, 4)) + _mod_piece(mod_ref, 3)).astype(BF16)
    p = _dot(xm_ref[...], win_ref[...])
    ca = p[:, 0:D_CONV]
    cg = p[:, D_CONV:2 * D_CONV]
    mid_ref[:, 0:D_CONV] = (ca * _sigmoid(cg)).astype(BF16)
    o = 2 * D_CONV
    q = p[:, o:o + D_RET]
    k = p[:, o + D_RET:o + 2 * D_RET]
    v = p[:, o + 2 * D_RET:o + 3 * D_RET]
    g = p[:, o + 3 * D_RET:o + 4 * D_RET]
    if is_lat:
        lane = lax.broadcasted_iota(jnp.int32, (tm, D_RET), 1)
        first_half = (lane % (HEAD_D // 2)) < (HEAD_D // 4)
        cos = jnp.concatenate([cos_ref[...]] * N_HEADS, axis=1)
        sin = jnp.concatenate([sin_ref[...]] * N_HEADS, axis=1)

        def rope(t):
            swapped = jnp.where(first_half, pltpu.roll(t, D_RET - HEAD_D // 4, 1),
                                pltpu.roll(t, HEAD_D // 4, 1))
            return t * cos + swapped * sin

        q = rope(q)
        k = rope(k)
    q = q * (HEAD_D ** -0.5)
    mid_ref[:, D_CONV:D_CONV + D_RET] = q.astype(BF16)
    mid_ref[:, D_CONV + D_RET:D_CONV + 2 * D_RET] = k.astype(BF16)
    vb = v.astype(BF16)
    mid_ref[:, D_CONV + 2 * D_RET:D_CONV + 3 * D_RET] = vb
    mid_ref[:, D_CONV + 3 * D_RET:D_CONV + 4 * D_RET] = (g * _sigmoid(g)).astype(BF16)

    n_chunks = tm // CHUNK
    if is_lat:
        @pl.when(step % tiles_per_seq == 0)
        def _():
            state_ref[...] = r0_ref[...]
    for h in range(N_HEADS):
        cols = slice(h * HEAD_D, (h + 1) * HEAD_D)
        r = state_ref[h] if is_lat else None
        for ci in reversed(range(n_chunks)):
            rows = slice(ci * CHUNK, (ci + 1) * CHUNK)
            if not is_lat and (ci + 1) % chunks_per_seq == 0:
                r = jnp.zeros((HEAD_D, HEAD_D), F32)
            rb_ref[ci, h] = r
            kd = (k[rows, cols] * kdec_ref[h]).astype(BF16)
            kv = lax.dot_general(kd, vb[rows, cols], (((0,), (0,)), ((), ())),
                                 preferred_element_type=F32)
            r = kdec_ref[N_HEADS + h] * r + kv
            if not is_lat and ci % chunks_per_seq == 0:
                rfin_ref[ci // chunks_per_seq, h] = r
        if is_lat:
            state_ref[h] = r


def _const_spec(shape, index):
    return pl.BlockSpec(shape, lambda i: index, pipeline_mode=pl.Buffered(1))


def _stage_a(layer, is_lat, alpha, x2d, seq_len, tm, lg, mod, ln_g, ln_b, w1, w2, w_in, rope_tabs, r0):
    n_tok = x2d.shape[0]
    nt = n_tok // tm
    tiles_per_seq = max(seq_len // tm, 1)
    n_chunks = tm // CHUNK
    n_seq = n_tok // seq_len
    rev = lambda i: nt - 1 - i

    if is_lat:
        group = lambda i: 1 + rev(i) // tiles_per_seq
    else:
        group = lambda i: 0
    in_specs = [
        pl.BlockSpec(memory_space=pltpu.SMEM),
        pl.BlockSpec((tm, D_MODEL), lambda i: (rev(i), 0)),
        pl.BlockSpec((None, None, 1, N_SUB * 3 * D_MODEL), lambda i: (layer, group(i), 0, 0)),
        _const_spec((None, N_SUB, D_MODEL), (layer, 0, 0)),
        _const_spec((None, N_SUB, D_MODEL), (layer, 0, 0)),
        _const_spec((None, None, D_MODEL, 2 * D_FF), (layer, 0, 0, 0)),
        _const_spec((None, None, D_FF, D_MODEL), (layer, 0, 0, 0)),
        _const_spec((None, D_MODEL, D_IN), (layer, 0, 0)),
    ]
    args = [lg, x2d, mod, ln_g, ln_b, w1, w2, w_in]
    out_shape = [
        jax.ShapeDtypeStruct((n_tok, D_MODEL), F32),
        jax.ShapeDtypeStruct((n_tok, D_MID), BF16),
        jax.ShapeDtypeStruct((n_tok // CHUNK, N_HEADS, HEAD_D, HEAD_D), F32),
    ]
    out_specs = [
        pl.BlockSpec((tm, D_MODEL), lambda i: (rev(i), 0)),
        pl.BlockSpec((tm, D_MID), lambda i: (rev(i), 0)),
        pl.BlockSpec((n_chunks, N_HEADS, HEAD_D, HEAD_D), lambda i: (rev(i), 0, 0, 0)),
    ]
    if is_lat:
        cos_t, sin_t = rope_tabs
        in_specs += [
            pl.BlockSpec((tm, HEAD_D), lambda i: (rev(i) % tiles_per_seq, 0)),
            pl.BlockSpec((tm, HEAD_D), lambda i: (rev(i) % tiles_per_seq, 0)),
            pl.BlockSpec((None, None, N_HEADS, HEAD_D, HEAD_D),
                         lambda i: (rev(i) // tiles_per_seq, layer, 0, 0, 0)),
        ]
        args += [cos_t, sin_t, r0]
    else:
        seqs_per_tile = tm // seq_len
        out_shape.append(jax.ShapeDtypeStruct((n_seq, N_HEADS, HEAD_D, HEAD_D), F32))
        out_specs.append(pl.BlockSpec((seqs_per_tile, N_HEADS, HEAD_D, HEAD_D),
                                      lambda i: (rev(i), 0, 0, 0)))
    return pl.pallas_call(
        functools.partial(_stage_a_kernel, layer, is_lat, tm, tiles_per_seq,
                          min(seq_len, tm) // CHUNK, alpha),
        grid=(nt,),
        in_specs=in_specs,
        out_specs=out_specs,
        out_shape=out_shape,
        scratch_shapes=[pltpu.VMEM((N_HEADS, HEAD_D, HEAD_D), F32),
                        pltpu.VMEM((2 * N_HEADS, CHUNK, HEAD_D), F32),
                        pltpu.VMEM((tm, D_MODEL), BF16),
                        pltpu.VMEM((tm, D_FF), BF16)],
        compiler_params=pltpu.CompilerParams(
            dimension_semantics=("arbitrary",), vmem_limit_bytes=VMEM_LIMIT),
        name=f"stage_a_{'lat' if is_lat else 'ctx'}_{layer}",
    )(*args)


def _stage_b_kernel(layer, is_lat, tm, nt, tiles_per_seq, chunks_per_seq, conv_dom, alpha,
                    *refs):
    if is_lat:
        (lg_ref, zero_ref, x1_ref, mid_ref, rb_ref, mod_ref, lng_ref, lnb_ref, cw_ref, cb_ref, clg_ref,
         clb_ref, wout_ref, w1_ref, w2_ref, r0_ref,
         out_ref, state_ref, tab_ref, upad_ref, shift_ref, mix_ref, xm_ref, h_ref) = refs
    else:
        (lg_ref, zero_ref, x1_ref, mid_ref, rb_ref, mod_ref, lng_ref, lnb_ref, cw_ref, cb_ref, clg_ref,
         clb_ref, wout_ref, w1_ref, w2_ref,
         out_ref, rfin_ref, state_ref, tab_ref, upad_ref, shift_ref, mix_ref, xm_ref, h_ref) = refs
    step = pl.program_id(0)
    n_dom = tm // conv_dom
    dom_rows = conv_dom + 2 * HALO
    copy_rows = COPY_ROWS
    n_chunks = tm // CHUNK

    @pl.when(step == 0)
    def _():
        ii = lax.broadcasted_iota(jnp.int32, (CHUNK, CHUNK), 0)
        jj = lax.broadcasted_iota(jnp.int32, (CHUNK, CHUNK), 1)
        dist = (ii - jj).astype(F32)
        for h in range(N_HEADS):
            lgf = lg_ref[layer * 2 * N_HEADS + h]
            lgb = lg_ref[layer * 2 * N_HEADS + N_HEADS + h]
            fwd = jnp.where(dist >= 0, jnp.exp(lgf * jnp.maximum(dist, 0.0)), 0.0)
            bwd = jnp.where(dist <= 0, jnp.exp(lgb * jnp.maximum(-dist, 0.0)), 0.0)
            tab_ref[h, 0] = fwd + bwd
            tab_ref[h, 1] = _decay_rows(lgf, 1.0, 1.0)
            tab_ref[h, 2] = _decay_rows(lgb, float(CHUNK), -1.0)
            tab_ref[h, 3] = _decay_rows(lgf, CHUNK - 1.0, -1.0)
            tab_ref[h, 4] = _decay_rows(lgf, float(CHUNK), 0.0)
        zeros = jnp.zeros((HALO, D_CONV), F32)
        for j in range(n_dom):
            upad_ref[j * dom_rows:j * dom_rows + HALO, :] = zeros
            upad_ref[j * dom_rows + HALO + conv_dom:(j + 1) * dom_rows, :] = zeros
        mix_ref[...] = jnp.zeros((tm, D_MODEL), BF16)

    if is_lat:
        @pl.when(jnp.minimum(step, nt - 1) % tiles_per_seq == 0)
        def _():
            state_ref[...] = r0_ref[...]

    units = []
    unit_cost = []

    def conv_fill(j, w, slot):
        def run():
            base = j * dom_rows
            if w == 0:
                upad_ref[base + HALO:base + HALO + conv_dom, :] = (
                    mid_ref[j * conv_dom:(j + 1) * conv_dom, 0:D_CONV].astype(F32))
            lo = base + w * CONV_WIN
            for b in range(1, SUBLANES):
                shift_ref[slot, b - 1] = upad_ref[lo + b:lo + b + copy_rows, :]
        return run

    def conv_block(j, w, r0, slot):
        def run():
            acc = jnp.broadcast_to(cb_ref[...], (CONV_ROWS, D_CONV))
            for d in range(CONV_K):
                off = HALO - CONV_PAD + d
                a, b = off // SUBLANES, off % SUBLANES
                if b == 0:
                    lo = j * dom_rows + w * CONV_WIN + a * SUBLANES + r0
                    tap = upad_ref[lo:lo + CONV_ROWS, :]
                else:
                    lo = a * SUBLANES + r0
                    tap = shift_ref[slot, b - 1, lo:lo + CONV_ROWS, :]
                acc = acc + cw_ref[d:d + 1, :] * tap
            uc = _layer_norm(acc, clg_ref[...], clb_ref[...])
            first = j * conv_dom + w * CONV_WIN + r0
            mix_ref[first:first + CONV_ROWS, 0:D_CONV] = (uc * _sigmoid(uc)).astype(BF16)
            return uc[0:TOKEN_ROWS, 0:HEAD_D]
        return run

    conv_units = []
    for j in range(n_dom):
        for w in range(conv_dom // CONV_WIN):
            slot = (j * (conv_dom // CONV_WIN) + w) % SHIFT_SLOTS
            fill = conv_fill(j, w, slot)
            for r0 in range(0, CONV_WIN, CONV_ROWS):
                conv_units.append((fill, conv_block(j, w, r0, slot)))
                fill = None

    def ret_unit(h, ci):
        qc = slice(D_CONV + h * HEAD_D, D_CONV + (h + 1) * HEAD_D)
        kc = slice(D_CONV + D_RET + h * HEAD_D, D_CONV + D_RET + (h + 1) * HEAD_D)
        vc = slice(D_CONV + 2 * D_RET + h * HEAD_D, D_CONV + 2 * D_RET + (h + 1) * HEAD_D)
        gc = slice(D_CONV + 3 * D_RET + h * HEAD_D, D_CONV + 3 * D_RET + (h + 1) * HEAD_D)

        def run():
            rows = slice(ci * CHUNK, (ci + 1) * CHUNK)
            if ci % chunks_per_seq == 0 and not is_lat:
                r = jnp.zeros((HEAD_D, HEAD_D), F32)
            else:
                r = state_ref[h]
            qh = mid_ref[rows, qc]
            kh = mid_ref[rows, kc]
            vh = mid_ref[rows, vc]
            s = lax.dot_general(qh, kh, (((1,), (1,)), ((), ())), preferred_element_type=F32)
            o = _dot((s * tab_ref[h, 0]).astype(BF16), vh)
            qf = qh.astype(F32)
            qd = jnp.concatenate([(qf * tab_ref[h, 1]).astype(BF16),
                                  (qf * tab_ref[h, 2]).astype(BF16)], axis=1)
            rcat = jnp.concatenate([r.astype(BF16), rb_ref[ci, h].astype(BF16)], axis=0)
            o = o + _dot(qd, rcat)
            kd = (kh.astype(F32) * tab_ref[h, 3]).astype(BF16)
            kv = lax.dot_general(kd, vh, (((0,), (0,)), ((), ())), preferred_element_type=F32)
            r = tab_ref[h, 4] * r + kv
            state_ref[h] = r
            if (ci + 1) % chunks_per_seq == 0 and not is_lat:
                rfin_ref[ci // chunks_per_seq, h] = r
            on = _layer_norm(o)
            mix_ref[rows, D_CONV + h * HEAD_D:D_CONV + (h + 1) * HEAD_D] = (
                on * mid_ref[rows, gc].astype(F32)).astype(BF16)
            return on[0:TOKEN_ROWS, :]
        return run

    ret_units = [ret_unit(h, ci) for ci in range(n_chunks) for h in range(N_HEADS)]

    for k in range(max(len(conv_units), len(ret_units))):
        if k < len(conv_units):
            fill, block = conv_units[k]
            if fill is not None:
                units.append(fill)
                unit_cost.append(3 * SUBLANES * CONV_WIN // CONV_ROWS)
            units.append(block)
            unit_cost.append(2 * CONV_K + 2 * SUBLANES)
        if k < len(ret_units):
            units.append(ret_units[k])
            unit_cost.append(3 * SUBLANES)

    n_phases = N_FF_CHUNKS + 2
    chunk_work = D_MODEL * 2 * FF_CHUNK
    phase_work = [D_MODEL * D_MODEL] + [chunk_work] * N_FF_CHUNKS + [chunk_work]
    starts = _split_units(unit_cost, phase_work)

    anchored = zero_ref[0] != 0

    def run_units(p):
        tok = None
        for u in units[starts[p]:starts[p + 1]]:
            piece = u()
            if piece is not None:
                piece = jnp.where(anchored, piece, 0.0)
                tok = piece if tok is None else tok + piece
        return tok

    def project():
        y = _dot(mix_ref[...], wout_ref[...])
        x2 = _layer_norm(alpha * x1_ref[...] + _mod_piece(mod_ref, 5) * y,
                         lng_ref[1:2, :], lnb_ref[1:2, :])
        out_ref[...] = x2
        xm_ref[...] = (x2 * (1.0 + _mod_piece(mod_ref, 7)) + _mod_piece(mod_ref, 6)).astype(BF16)

    def fold_into_hidden(c, tok):
        if tok is not None:
            cols = slice(c * FF_CHUNK, c * FF_CHUNK + HEAD_D)
            h_ref[0:TOKEN_ROWS, cols] = (h_ref[0:TOKEN_ROWS, cols].astype(F32) + tok).astype(BF16)

    project()
    fold_first = run_units(0)
    for c in range(N_FF_CHUNKS):
        _ffn_hidden_chunk(c, xm_ref, w1_ref, h_ref)
        if c == 0:
            fold_into_hidden(0, fold_first)
        fold_into_hidden(c, run_units(c + 1))
    y2 = _dot(h_ref[...], w2_ref[...])
    tail = run_units(n_phases - 1)
    out = _layer_norm(alpha * out_ref[...] + 0.5 * _mod_piece(mod_ref, 8) * y2,
                      lng_ref[2:3, :], lnb_ref[2:3, :])
    out_ref[...] = out
    if tail is not None:
        out_ref[0:TOKEN_ROWS, 0:HEAD_D] = out[0:TOKEN_ROWS, 0:HEAD_D] + tail


def _stage_b(layer, is_lat, alpha, x1, mid, rb, seq_len, tm, lg, mod, ln_g, ln_b, conv_w, conv_b, conv_ln_g,
             conv_ln_b, w_out, w1, w2, r0):
    n_tok = x1.shape[0]
    nt = n_tok // tm
    tiles_per_seq = max(seq_len // tm, 1)
    n_chunks = tm // CHUNK
    n_seq = n_tok // seq_len
    conv_dom = GRID_W if is_lat else seq_len
    n_dom = tm // conv_dom

    ftile = lambda i: jnp.maximum(i - 1, 0)
    mtile = lambda i: jnp.minimum(i, nt - 1)
    if is_lat:
        group = lambda i: 1 + ftile(i) // tiles_per_seq
    else:
        group = lambda i: 0
    in_specs = [
        pl.BlockSpec(memory_space=pltpu.SMEM),
        pl.BlockSpec(memory_space=pltpu.SMEM),
        pl.BlockSpec((tm, D_MODEL), lambda i: (ftile(i), 0)),
        pl.BlockSpec((tm, D_MID), lambda i: (mtile(i), 0)),
        pl.BlockSpec((n_chunks, N_HEADS, HEAD_D, HEAD_D), lambda i: (mtile(i), 0, 0, 0)),
        pl.BlockSpec((None, None, 1, N_SUB * 3 * D_MODEL), lambda i: (layer, group(i), 0, 0)),
        _const_spec((None, N_SUB, D_MODEL), (layer, 0, 0)),
        _const_spec((None, N_SUB, D_MODEL), (layer, 0, 0)),
        _const_spec((None, CONV_K, D_CONV), (layer, 0, 0)),
        _const_spec((None, 1, D_CONV), (layer, 0, 0)),
        _const_spec((None, 1, D_CONV), (layer, 0, 0)),
        _const_spec((None, 1, D_CONV), (layer, 0, 0)),
        _const_spec((None, D_MODEL, D_MODEL), (layer, 0, 0)),
        _const_spec((None, None, D_MODEL, 2 * D_FF), (layer, 1, 0, 0)),
        _const_spec((None, None, D_FF, D_MODEL), (layer, 1, 0, 0)),
    ]
    args = [lg, jnp.zeros((1,), jnp.int32), x1, mid, rb, mod, ln_g, ln_b, conv_w, conv_b, conv_ln_g,
            conv_ln_b, w_out, w1, w2]
    out_shape = [jax.ShapeDtypeStruct((n_tok, D_MODEL), F32)]
    out_specs = [pl.BlockSpec((tm, D_MODEL), lambda i: (ftile(i), 0))]
    if is_lat:
        in_specs.append(pl.BlockSpec((None, None, N_HEADS, HEAD_D, HEAD_D),
                                     lambda i: (mtile(i) // tiles_per_seq, layer, 0, 0, 0)))
        args.append(r0)
    else:
        seqs_per_tile = tm // seq_len
        out_shape.append(jax.ShapeDtypeStruct((n_seq, N_HEADS, HEAD_D, HEAD_D), F32))
        out_specs.append(pl.BlockSpec((seqs_per_tile, N_HEADS, HEAD_D, HEAD_D),
                                      lambda i: (mtile(i), 0, 0, 0)))
    return pl.pallas_call(
        functools.partial(_stage_b_kernel, layer, is_lat, tm, nt, tiles_per_seq,
                          min(seq_len, tm) // CHUNK, conv_dom, alpha),
        grid=(nt + 1,),
        in_specs=in_specs,
        out_specs=out_specs,
        out_shape=out_shape,
        scratch_shapes=[pltpu.VMEM((N_HEADS, HEAD_D, HEAD_D), F32),
                        pltpu.VMEM((N_HEADS, 5, CHUNK, CHUNK), F32),
                        pltpu.VMEM((n_dom * (conv_dom + 2 * HALO), D_CONV), F32),
                        pltpu.VMEM((SHIFT_SLOTS, SUBLANES - 1, COPY_ROWS, D_CONV), F32),
                        pltpu.VMEM((tm, D_MODEL), BF16),
                        pltpu.VMEM((tm, D_MODEL), BF16),
                        pltpu.VMEM((tm, D_FF), BF16)],
        compiler_params=pltpu.CompilerParams(
            dimension_semantics=("arbitrary",), vmem_limit_bytes=VMEM_LIMIT),
        name=f"stage_b_{'lat' if is_lat else 'ctx'}_{layer}",
    )(*args)


def _rope_tables(seq_len):
    t = jnp.arange(seq_len)
    r = (t // GRID_W).astype(F32)
    col = (t % GRID_W).astype(F32)
    nf = HEAD_D // 4
    inv = ROPE_BASE ** (-jnp.arange(nf, dtype=F32) / nf)
    ang_row = r[:, None] * inv
    ang_col = col[:, None] * inv
    cos = jnp.concatenate([jnp.cos(ang_row)] * 2 + [jnp.cos(ang_col)] * 2, axis=1)
    sin = jnp.concatenate([-jnp.sin(ang_row), jnp.sin(ang_row), -jnp.sin(ang_col), jnp.sin(ang_col)],
                          axis=1)
    return cos, sin


def kernel(x_prompt, x_sample, state_ret_fwd, state_ret_bwd, c, c_ctx, w_ada, b_ada, ln_g, ln_b, ffn_w1,
           ffn_w2, w_in, w_out, conv_w, conv_b, conv_ln_g, conv_ln_b, ret_decay_logit):
    batch, seq, _ = x_prompt.shape
    dec_batch, dec_seq, _ = x_sample.shape
    depth = w_ada.shape[0]
    alpha = (2.0 * depth) ** 0.25
    assert dec_batch + 1 <= COND_ROWS and seq % CHUNK == 0 and dec_seq % CHUNK == 0

    cond_all = jnp.zeros((COND_ROWS, D_MODEL), F32).at[0].set(c_ctx).at[1:1 + dec_batch].set(c)
    mod = _modulation(cond_all, w_ada, b_ada).reshape(depth, COND_ROWS, 1, N_SUB * 3 * D_MODEL)
    lg = jax.nn.log_sigmoid(ret_decay_logit.astype(F32)).reshape(-1)
    rope_tabs = _rope_tables(dec_seq)
    w1b = ffn_w1.astype(BF16)
    w2b = ffn_w2.astype(BF16)
    winb = w_in.astype(BF16)
    woutb = w_out.astype(BF16)
    conv_b3 = conv_b.reshape(depth, 1, D_CONV)
    clg3 = conv_ln_g.reshape(depth, 1, D_CONV)
    clb3 = conv_ln_b.reshape(depth, 1, D_CONV)

    tm_ctx = min(512, batch * seq)
    tm_lat = min(512, dec_seq)
    ctx = x_prompt.reshape(batch * seq, D_MODEL)
    lat = x_sample.reshape(dec_batch * dec_seq, D_MODEL)
    new_fwd = []
    new_bwd = []
    for l in range(depth):
        x1, mid, rb, r_b = _stage_a(l, False, alpha, ctx, seq, tm_ctx, lg, mod, ln_g, ln_b, w1b, w2b,
                                    winb, None, None)
        ctx, r_f = _stage_b(l, False, alpha, x1, mid, rb, seq, tm_ctx, lg, mod, ln_g, ln_b, conv_w,
                            conv_b3, clg3, clb3, woutb, w1b, w2b, None)
        new_fwd.append(r_f)
        new_bwd.append(r_b)
        x1, mid, rb = _stage_a(l, True, alpha, lat, dec_seq, tm_lat, lg, mod, ln_g, ln_b, w1b, w2b,
                               winb, rope_tabs, state_ret_bwd)
        (lat,) = _stage_b(l, True, alpha, x1, mid, rb, dec_seq, tm_lat, lg, mod, ln_g, ln_b, conv_w,
                          conv_b3, clg3, clb3, woutb, w1b, w2b, state_ret_fwd)
    return (ctx.reshape(batch, seq, D_MODEL), lat.reshape(dec_batch, dec_seq, D_MODEL),
            jnp.stack(new_fwd, axis=1), jnp.stack(new_bwd, axis=1))
```

```python
import functools

import jax
import jax.numpy as jnp
from jax import lax
from jax.experimental import pallas as pl
from jax.experimental.pallas import tpu as pltpu

D_MODEL = 1024
D_CONV = 512
D_RET = 512
N_HEADS = 4
HEAD_D = 128
CONV_K = 31
CONV_PAD = 15
CHUNK = 128
GRID_W = 64
D_FF = 2816
N_SUB = 3
D_IN = 2 * D_CONV + 4 * D_RET
D_MID = D_CONV + 4 * D_RET
ROPE_BASE = 10000.0
EPS = 1e-5
SUBLANES = 8
TOKEN_ROWS = 16
COND_ROWS = 16
CONV_ROWS = 32
CONV_WIN = 64
COPY_ROWS = CONV_WIN + (CONV_K // SUBLANES) * SUBLANES
SHIFT_SLOTS = 2
HALO = 16
FF_CHUNK = 256
N_FF_CHUNKS = D_FF // FF_CHUNK
VMEM_LIMIT = 56 * 1024 * 1024

F32 = jnp.float32
BF16 = jnp.bfloat16


def _dot(a, b):
    return jnp.dot(a, b, preferred_element_type=F32)


def _sigmoid(x):
    return 1.0 / (1.0 + jnp.exp(-x))


def _layer_norm(x, g=None, b=None):
    mu = jnp.mean(x, axis=-1, keepdims=True)
    xc = x - mu
    var = jnp.mean(xc * xc, axis=-1, keepdims=True)
    y = xc * lax.rsqrt(var + EPS)
    if g is not None:
        y = y * g + b
    return y


def _ffn_hidden_chunk(c, xm_ref, w1_ref, h_ref, rows=slice(None)):
    lo = c * FF_CHUNK
    a = _dot(xm_ref[rows, :], w1_ref[:, lo:lo + FF_CHUNK])
    b = _dot(xm_ref[rows, :], w1_ref[:, D_FF + lo:D_FF + lo + FF_CHUNK])
    h_ref[:, lo:lo + FF_CHUNK] = (a * _sigmoid(a) * b).astype(BF16)


def _ffn(xm_ref, w1_ref, w2_ref, h_ref):
    for c in range(N_FF_CHUNKS):
        _ffn_hidden_chunk(c, xm_ref, w1_ref, h_ref)
    return _dot(h_ref[...], w2_ref[...])


def _split_units(costs, work):
    total_cost, total_work = sum(costs), sum(work)
    starts, k, spent, done = [0], 0, 0, 0
    for w in work[:-1]:
        done += w
        while k < len(costs) and (spent + costs[k] / 2) * total_work <= done * total_cost:
            spent += costs[k]
            k += 1
        starts.append(k)
    starts.append(len(costs))
    return starts


def _mod_piece(mod_ref, i):
    return mod_ref[:, i * D_MODEL:(i + 1) * D_MODEL]


def _decay_rows(lg, offset, sign):
    i = lax.broadcasted_iota(jnp.int32, (CHUNK, HEAD_D), 0).astype(F32)
    return jnp.exp(lg * (offset + sign * i))


def _mod_kernel(cond_ref, w_ref, b_ref, out_ref):
    cnd = cond_ref[...]
    s = (cnd * _sigmoid(cnd)).astype(BF16)
    out_ref[...] = _dot(s, w_ref[...].astype(BF16)) + b_ref[...]


def _modulation(cond_all, w_ada, b_ada):
    depth = w_ada.shape[0]
    n_out = w_ada.shape[2]
    tn = n_out // 4
    return pl.pallas_call(
        _mod_kernel,
        grid=(depth, n_out // tn),
        in_specs=[
            pl.BlockSpec((COND_ROWS, D_MODEL), lambda l, j: (0, 0)),
            pl.BlockSpec((None, D_MODEL, tn), lambda l, j: (l, 0, j)),
            pl.BlockSpec((None, 1, tn), lambda l, j: (l, 0, j)),
        ],
        out_specs=pl.BlockSpec((None, COND_ROWS, tn), lambda l, j: (l, 0, j)),
        out_shape=jax.ShapeDtypeStruct((depth, COND_ROWS, n_out), F32),
        compiler_params=pltpu.CompilerParams(
            dimension_semantics=("arbitrary", "arbitrary"), vmem_limit_bytes=VMEM_LIMIT),
        name="adaln_modulation",
    )(cond_all, w_ada, b_ada.reshape(depth, 1, n_out))


def _drop_ref(refs, pos):
    return refs if pos is None else refs[:pos] + refs[pos + 1:]


def _stage_a_kernel(layer, is_lat, tm, tiles_per_seq, chunks_per_seq, alpha, acc_pos, *refs):
    refs = _drop_ref(refs, acc_pos)
    if is_lat:
        (lg_ref, x_ref, mod_ref, lng_ref, lnb_ref, w1_ref, w2_ref, win_ref, cos_ref, sin_ref, r0_ref,
         x1_ref, mid_ref, rb_ref, state_ref, kdec_ref, xm_ref, h_ref) = refs
    else:
        (lg_ref, x_ref, mod_ref, lng_ref, lnb_ref, w1_ref, w2_ref, win_ref,
         x1_ref, mid_ref, rb_ref, rfin_ref, state_ref, kdec_ref, xm_ref, h_ref) = refs
    step = pl.program_id(0)

    @pl.when(step == 0)
    def _():
        for h in range(N_HEADS):
            lg = lg_ref[layer * 2 * N_HEADS + N_HEADS + h]
            kdec_ref[h] = _decay_rows(lg, 0.0, 1.0)
            kdec_ref[N_HEADS + h] = _decay_rows(lg, float(CHUNK), 0.0)

    xm_ref[...] = (x_ref[...] * (1.0 + _mod_piece(mod_ref, 1)) + _mod_piece(mod_ref, 0)).astype(BF16)
    y = _ffn(xm_ref, w1_ref, w2_ref, h_ref)
    x1 = _layer_norm(alpha * x_ref[...] + 0.5 * _mod_piece(mod_ref, 2) * y,
                     lng_ref[0:1, :], lnb_ref[0:1, :])
    x1_ref[...] = x1

    xm_ref[...] = (x1 * (1.0 + _mod_piece(mod_ref, 4)) + _mod_piece(mod_ref, 3)).astype(BF16)
    p = _dot(xm_ref[...], win_ref[...])
    ca = p[:, 0:D_CONV]
    cg = p[:, D_CONV:2 * D_CONV]
    mid_ref[:, 0:D_CONV] = (ca * _sigmoid(cg)).astype(BF16)
    o = 2 * D_CONV
    q = p[:, o:o + D_RET]
    k = p[:, o + D_RET:o + 2 * D_RET]
    v = p[:, o + 2 * D_RET:o + 3 * D_RET]
    g = p[:, o + 3 * D_RET:o + 4 * D_RET]
    if is_lat:
        lane = lax.broadcasted_iota(jnp.int32, (tm, D_RET), 1)
        first_half = (lane % (HEAD_D // 2)) < (HEAD_D // 4)
        cos = jnp.concatenate([cos_ref[...]] * N_HEADS, axis=1)
        sin = jnp.concatenate([sin_ref[...]] * N_HEADS, axis=1)

        def rope(t):
            swapped = jnp.where(first_half, pltpu.roll(t, D_RET - HEAD_D // 4, 1),
                                pltpu.roll(t, HEAD_D // 4, 1))
            return t * cos + swapped * sin

        q = rope(q)
        k = rope(k)
    q = q * (HEAD_D ** -0.5)
    mid_ref[:, D_CONV:D_CONV + D_RET] = q.astype(BF16)
    mid_ref[:, D_CONV + D_RET:D_CONV + 2 * D_RET] = k.astype(BF16)
    vb = v.astype(BF16)
    mid_ref[:, D_CONV + 2 * D_RET:D_CONV + 3 * D_RET] = vb
    mid_ref[:, D_CONV + 3 * D_RET:D_CONV + 4 * D_RET] = (g * _sigmoid(g)).astype(BF16)

    n_chunks = tm // CHUNK
    if is_lat:
        @pl.when(step % tiles_per_seq == 0)
        def _():
            state_ref[...] = r0_ref[...]
    for h in range(N_HEADS):
        cols = slice(h * HEAD_D, (h + 1) * HEAD_D)
        r = state_ref[h] if is_lat else None
        for ci in reversed(range(n_chunks)):
            rows = slice(ci * CHUNK, (ci + 1) * CHUNK)
            if not is_lat and (ci + 1) % chunks_per_seq == 0:
                r = jnp.zeros((HEAD_D, HEAD_D), F32)
            rb_ref[ci, h] = r
            kd = (k[rows, cols] * kdec_ref[h]).astype(BF16)
            kv = lax.dot_general(kd, vb[rows, cols], (((0,), (0,)), ((), ())),
                                 preferred_element_type=F32)
            r = kdec_ref[N_HEADS + h] * r + kv
            if not is_lat and ci % chunks_per_seq == 0:
                rfin_ref[ci // chunks_per_seq, h] = r
        if is_lat:
            state_ref[h] = r


def _const_spec(shape, index):
    return pl.BlockSpec(shape, lambda i: index, pipeline_mode=pl.Buffered(1))


def _stage_a(layer, is_lat, alpha, x2d, seq_len, tm, lg, mod, ln_g, ln_b, w1, w2, w_in, rope_tabs, r0,
             acc=None):
    depth = w1.shape[0]
    acc_pos, aliases = None, {}
    n_tok = x2d.shape[0]
    nt = n_tok // tm
    tiles_per_seq = max(seq_len // tm, 1)
    n_chunks = tm // CHUNK
    n_seq = n_tok // seq_len
    rev = lambda i: nt - 1 - i

    if is_lat:
        group = lambda i: 1 + rev(i) // tiles_per_seq
    else:
        group = lambda i: 0
    in_specs = [
        pl.BlockSpec(memory_space=pltpu.SMEM),
        pl.BlockSpec((tm, D_MODEL), lambda i: (rev(i), 0)),
        pl.BlockSpec((None, None, 1, N_SUB * 3 * D_MODEL), lambda i: (layer, group(i), 0, 0)),
        _const_spec((None, N_SUB, D_MODEL), (layer, 0, 0)),
        _const_spec((None, N_SUB, D_MODEL), (layer, 0, 0)),
        _const_spec((None, None, D_MODEL, 2 * D_FF), (layer, 0, 0, 0)),
        _const_spec((None, None, D_FF, D_MODEL), (layer, 0, 0, 0)),
        _const_spec((None, D_MODEL, D_IN), (layer, 0, 0)),
    ]
    args = [lg, x2d, mod, ln_g, ln_b, w1, w2, w_in]
    out_shape = [
        jax.ShapeDtypeStruct((n_tok, D_MODEL), F32),
        jax.ShapeDtypeStruct((n_tok, D_MID), BF16),
        jax.ShapeDtypeStruct((n_tok // CHUNK, N_HEADS, HEAD_D, HEAD_D), F32),
    ]
    out_specs = [
        pl.BlockSpec((tm, D_MODEL), lambda i: (rev(i), 0)),
        pl.BlockSpec((tm, D_MID), lambda i: (rev(i), 0)),
        pl.BlockSpec((n_chunks, N_HEADS, HEAD_D, HEAD_D), lambda i: (rev(i), 0, 0, 0)),
    ]
    if is_lat:
        cos_t, sin_t = rope_tabs
        in_specs += [
            pl.BlockSpec((tm, HEAD_D), lambda i: (rev(i) % tiles_per_seq, 0)),
            pl.BlockSpec((tm, HEAD_D), lambda i: (rev(i) % tiles_per_seq, 0)),
            pl.BlockSpec((None, None, N_HEADS, HEAD_D, HEAD_D),
                         lambda i: (rev(i) // tiles_per_seq, layer, 0, 0, 0)),
        ]
        args += [cos_t, sin_t, r0]
    else:
        seqs_per_tile = tm // seq_len
        out_shape.append(jax.ShapeDtypeStruct((n_seq, depth, N_HEADS, HEAD_D, HEAD_D), F32))
        out_specs.append(pl.BlockSpec((seqs_per_tile, None, N_HEADS, HEAD_D, HEAD_D),
                                      lambda i: (rev(i), layer, 0, 0, 0)))
        if acc is not None:
            acc_pos = len(args)
            aliases = {acc_pos: len(out_shape) - 1}
            in_specs.append(pl.BlockSpec(memory_space=pl.ANY))
            args.append(acc)
    return pl.pallas_call(
        functools.partial(_stage_a_kernel, layer, is_lat, tm, tiles_per_seq,
                          min(seq_len, tm) // CHUNK, alpha, acc_pos),
        grid=(nt,),
        in_specs=in_specs,
        out_specs=out_specs,
        out_shape=out_shape,
        input_output_aliases=aliases,
        scratch_shapes=[pltpu.VMEM((N_HEADS, HEAD_D, HEAD_D), F32),
                        pltpu.VMEM((2 * N_HEADS, CHUNK, HEAD_D), F32),
                        pltpu.VMEM((tm, D_MODEL), BF16),
                        pltpu.VMEM((tm, D_FF), BF16)],
        compiler_params=pltpu.CompilerParams(
            dimension_semantics=("arbitrary",), vmem_limit_bytes=VMEM_LIMIT),
        name=f"stage_a_{'lat' if is_lat else 'ctx'}_{layer}",
    )(*args)


def _stage_b_kernel(layer, is_lat, tm, nt, tiles_per_seq, chunks_per_seq, conv_dom, alpha, acc_pos,
                    *refs):
    refs = _drop_ref(refs, acc_pos)
    if is_lat:
        (lg_ref, zero_ref, x1_ref, mid_ref, rb_ref, mod_ref, lng_ref, lnb_ref, cw_ref, cb_ref, clg_ref,
         clb_ref, wout_ref, w1_ref, w2_ref, r0_ref,
         out_ref, state_ref, tab_ref, upad_ref, shift_ref, mix_ref, xm_ref, h_ref) = refs
    else:
        (lg_ref, zero_ref, x1_ref, mid_ref, rb_ref, mod_ref, lng_ref, lnb_ref, cw_ref, cb_ref, clg_ref,
         clb_ref, wout_ref, w1_ref, w2_ref,
         out_ref, rfin_ref, state_ref, tab_ref, upad_ref, shift_ref, mix_ref, xm_ref, h_ref) = refs
    step = pl.program_id(0)
    n_dom = tm // conv_dom
    dom_rows = conv_dom + 2 * HALO
    copy_rows = COPY_ROWS
    n_chunks = tm // CHUNK

    @pl.when(step == 0)
    def _():
        ii = lax.broadcasted_iota(jnp.int32, (CHUNK, CHUNK), 0)
        jj = lax.broadcasted_iota(jnp.int32, (CHUNK, CHUNK), 1)
        dist = (ii - jj).astype(F32)
        for h in range(N_HEADS):
            lgf = lg_ref[layer * 2 * N_HEADS + h]
            lgb = lg_ref[layer * 2 * N_HEADS + N_HEADS + h]
            fwd = jnp.where(dist >= 0, jnp.exp(lgf * jnp.maximum(dist, 0.0)), 0.0)
            bwd = jnp.where(dist <= 0, jnp.exp(lgb * jnp.maximum(-dist, 0.0)), 0.0)
            tab_ref[h, 0] = fwd + bwd
            tab_ref[h, 1] = _decay_rows(lgf, 1.0, 1.0)
            tab_ref[h, 2] = _decay_rows(lgb, float(CHUNK), -1.0)
            tab_ref[h, 3] = _decay_rows(lgf, CHUNK - 1.0, -1.0)
            tab_ref[h, 4] = _decay_rows(lgf, float(CHUNK), 0.0)
        zeros = jnp.zeros((HALO, D_CONV), F32)
        for j in range(n_dom):
            upad_ref[j * dom_rows:j * dom_rows + HALO, :] = zeros
            upad_ref[j * dom_rows + HALO + conv_dom:(j + 1) * dom_rows, :] = zeros
        mix_ref[...] = jnp.zeros((tm, D_MODEL), BF16)

    if is_lat:
        @pl.when(jnp.minimum(step, nt - 1) % tiles_per_seq == 0)
        def _():
            state_ref[...] = r0_ref[...]

    units = []
    unit_cost = []

    def conv_fill(j, w, slot):
        def run():
            base = j * dom_rows
            if w == 0:
                upad_ref[base + HALO:base + HALO + conv_dom, :] = (
                    mid_ref[j * conv_dom:(j + 1) * conv_dom, 0:D_CONV].astype(F32))
            lo = base + w * CONV_WIN
            for b in range(1, SUBLANES):
                shift_ref[slot, b - 1] = upad_ref[lo + b:lo + b + copy_rows, :]
        return run

    def conv_block(j, w, r0, slot):
        def run():
            acc = jnp.broadcast_to(cb_ref[...], (CONV_ROWS, D_CONV))
            for d in range(CONV_K):
                off = HALO - CONV_PAD + d
                a, b = off // SUBLANES, off % SUBLANES
                if b == 0:
                    lo = j * dom_rows + w * CONV_WIN + a * SUBLANES + r0
                    tap = upad_ref[lo:lo + CONV_ROWS, :]
                else:
                    lo = a * SUBLANES + r0
                    tap = shift_ref[slot, b - 1, lo:lo + CONV_ROWS, :]
                acc = acc + cw_ref[d:d + 1, :] * tap
            uc = _layer_norm(acc, clg_ref[...], clb_ref[...])
            first = j * conv_dom + w * CONV_WIN + r0
            mix_ref[first:first + CONV_ROWS, 0:D_CONV] = (uc * _sigmoid(uc)).astype(BF16)
            return uc[0:TOKEN_ROWS, 0:HEAD_D]
        return run

    conv_units = []
    for j in range(n_dom):
        for w in range(conv_dom // CONV_WIN):
            slot = (j * (conv_dom // CONV_WIN) + w) % SHIFT_SLOTS
            fill = conv_fill(j, w, slot)
            for r0 in range(0, CONV_WIN, CONV_ROWS):
                conv_units.append((fill, conv_block(j, w, r0, slot)))
                fill = None

    def ret_unit(h, ci):
        qc = slice(D_CONV + h * HEAD_D, D_CONV + (h + 1) * HEAD_D)
        kc = slice(D_CONV + D_RET + h * HEAD_D, D_CONV + D_RET + (h + 1) * HEAD_D)
        vc = slice(D_CONV + 2 * D_RET + h * HEAD_D, D_CONV + 2 * D_RET + (h + 1) * HEAD_D)
        gc = slice(D_CONV + 3 * D_RET + h * HEAD_D, D_CONV + 3 * D_RET + (h + 1) * HEAD_D)

        def run():
            rows = slice(ci * CHUNK, (ci + 1) * CHUNK)
            if ci % chunks_per_seq == 0 and not is_lat:
                r = jnp.zeros((HEAD_D, HEAD_D), F32)
            else:
                r = state_ref[h]
            qh = mid_ref[rows, qc]
            kh = mid_ref[rows, kc]
            vh = mid_ref[rows, vc]
            s = lax.dot_general(qh, kh, (((1,), (1,)), ((), ())), preferred_element_type=F32)
            o = _dot((s * tab_ref[h, 0]).astype(BF16), vh)
            qf = qh.astype(F32)
            qd = jnp.concatenate([(qf * tab_ref[h, 1]).astype(BF16),
                                  (qf * tab_ref[h, 2]).astype(BF16)], axis=1)
            rcat = jnp.concatenate([r.astype(BF16), rb_ref[ci, h].astype(BF16)], axis=0)
            o = o + _dot(qd, rcat)
            kd = (kh.astype(F32) * tab_ref[h, 3]).astype(BF16)
            kv = lax.dot_general(kd, vh, (((0,), (0,)), ((), ())), preferred_element_type=F32)
            r = tab_ref[h, 4] * r + kv
            state_ref[h] = r
            if (ci + 1) % chunks_per_seq == 0 and not is_lat:
                rfin_ref[ci // chunks_per_seq, h] = r
            on = _layer_norm(o)
            mix_ref[rows, D_CONV + h * HEAD_D:D_CONV + (h + 1) * HEAD_D] = (
                on * mid_ref[rows, gc].astype(F32)).astype(BF16)
            return on[0:TOKEN_ROWS, :]
        return run

    ret_units = [ret_unit(h, ci) for ci in range(n_chunks) for h in range(N_HEADS)]

    for k in range(max(len(conv_units), len(ret_units))):
        if k < len(conv_units):
            fill, block = conv_units[k]
            if fill is not None:
                units.append(fill)
                unit_cost.append(3 * SUBLANES * CONV_WIN // CONV_ROWS)
            units.append(block)
            unit_cost.append(2 * CONV_K + 2 * SUBLANES)
        if k < len(ret_units):
            units.append(ret_units[k])
            unit_cost.append(3 * SUBLANES)

    n_phases = N_FF_CHUNKS + 2
    chunk_work = D_MODEL * 2 * FF_CHUNK
    phase_work = [D_MODEL * D_MODEL] + [chunk_work] * N_FF_CHUNKS + [chunk_work]
    starts = _split_units(unit_cost, phase_work)

    anchored = zero_ref[0] != 0

    def run_units(p):
        tok = None
        for u in units[starts[p]:starts[p + 1]]:
            piece = u()
            if piece is not None:
                piece = jnp.where(anchored, piece, 0.0)
                tok = piece if tok is None else tok + piece
        return tok

    def fold_into(ref, cols, tok):
        if tok is not None:
            ref[0:TOKEN_ROWS, cols] = (ref[0:TOKEN_ROWS, cols].astype(F32) + tok).astype(BF16)

    y = _dot(mix_ref[...], wout_ref[...])
    x2 = _layer_norm(alpha * x1_ref[...] + _mod_piece(mod_ref, 5) * y, lng_ref[1:2, :], lnb_ref[1:2, :])
    out_ref[...] = x2
    xm_ref[...] = (x2 * (1.0 + _mod_piece(mod_ref, 7)) + _mod_piece(mod_ref, 6)).astype(BF16)
    first = run_units(0)
    for c in range(N_FF_CHUNKS):
        _ffn_hidden_chunk(c, xm_ref, w1_ref, h_ref)
        if c == 0:
            fold_into(h_ref, slice(0, HEAD_D), first)
        fold_into(h_ref, slice(c * FF_CHUNK, c * FF_CHUNK + HEAD_D), run_units(c + 1))
    y2 = _dot(h_ref[...], w2_ref[...])
    tail = run_units(n_phases - 1)
    out = _layer_norm(alpha * out_ref[...] + 0.5 * _mod_piece(mod_ref, 8) * y2,
                      lng_ref[2:3, :], lnb_ref[2:3, :])
    out_ref[...] = out
    if tail is not None:
        out_ref[0:TOKEN_ROWS, 0:HEAD_D] = out[0:TOKEN_ROWS, 0:HEAD_D] + tail


def _stage_b(layer, is_lat, alpha, x1, mid, rb, seq_len, tm, lg, mod, ln_g, ln_b, conv_w, conv_b, conv_ln_g,
             conv_ln_b, w_out, w1, w2, r0, acc=None):
    depth = w1.shape[0]
    acc_pos, aliases = None, {}
    n_tok = x1.shape[0]
    nt = n_tok // tm
    tiles_per_seq = max(seq_len // tm, 1)
    n_chunks = tm // CHUNK
    n_seq = n_tok // seq_len
    conv_dom = GRID_W if is_lat else seq_len
    n_dom = tm // conv_dom

    ftile = lambda i: jnp.maximum(i - 1, 0)
    mtile = lambda i: jnp.minimum(i, nt - 1)
    if is_lat:
        group = lambda i: 1 + ftile(i) // tiles_per_seq
    else:
        group = lambda i: 0
    in_specs = [
        pl.BlockSpec(memory_space=pltpu.SMEM),
        pl.BlockSpec(memory_space=pltpu.SMEM),
        pl.BlockSpec((tm, D_MODEL), lambda i: (ftile(i), 0)),
        pl.BlockSpec((tm, D_MID), lambda i: (mtile(i), 0)),
        pl.BlockSpec((n_chunks, N_HEADS, HEAD_D, HEAD_D), lambda i: (mtile(i), 0, 0, 0)),
        pl.BlockSpec((None, None, 1, N_SUB * 3 * D_MODEL), lambda i: (layer, group(i), 0, 0)),
        _const_spec((None, N_SUB, D_MODEL), (layer, 0, 0)),
        _const_spec((None, N_SUB, D_MODEL), (layer, 0, 0)),
        _const_spec((None, CONV_K, D_CONV), (layer, 0, 0)),
        _const_spec((None, 1, D_CONV), (layer, 0, 0)),
        _const_spec((None, 1, D_CONV), (layer, 0, 0)),
        _const_spec((None, 1, D_CONV), (layer, 0, 0)),
        _const_spec((None, D_MODEL, D_MODEL), (layer, 0, 0)),
        _const_spec((None, None, D_MODEL, 2 * D_FF), (layer, 1, 0, 0)),
        _const_spec((None, None, D_FF, D_MODEL), (layer, 1, 0, 0)),
    ]
    args = [lg, jnp.zeros((1,), jnp.int32), x1, mid, rb, mod, ln_g, ln_b, conv_w, conv_b, conv_ln_g,
            conv_ln_b, w_out, w1, w2]
    out_shape = [jax.ShapeDtypeStruct((n_tok, D_MODEL), F32)]
    out_specs = [pl.BlockSpec((tm, D_MODEL), lambda i: (ftile(i), 0))]
    if is_lat:
        in_specs.append(pl.BlockSpec((None, None, N_HEADS, HEAD_D, HEAD_D),
                                     lambda i: (mtile(i) // tiles_per_seq, layer, 0, 0, 0)))
        args.append(r0)
    else:
        seqs_per_tile = tm // seq_len
        out_shape.append(jax.ShapeDtypeStruct((n_seq, depth, N_HEADS, HEAD_D, HEAD_D), F32))
        out_specs.append(pl.BlockSpec((seqs_per_tile, None, N_HEADS, HEAD_D, HEAD_D),
                                      lambda i: (mtile(i), layer, 0, 0, 0)))
        if acc is not None:
            acc_pos = len(args)
            aliases = {acc_pos: len(out_shape) - 1}
            in_specs.append(pl.BlockSpec(memory_space=pl.ANY))
            args.append(acc)
    return pl.pallas_call(
        functools.partial(_stage_b_kernel, layer, is_lat, tm, nt, tiles_per_seq,
                          min(seq_len, tm) // CHUNK, conv_dom, alpha, acc_pos),
        grid=(nt + 1,),
        in_specs=in_specs,
        out_specs=out_specs,
        out_shape=out_shape,
        input_output_aliases=aliases,
        scratch_shapes=[pltpu.VMEM((N_HEADS, HEAD_D, HEAD_D), F32),
                        pltpu.VMEM((N_HEADS, 5, CHUNK, CHUNK), F32),
                        pltpu.VMEM((n_dom * (conv_dom + 2 * HALO), D_CONV), F32),
                        pltpu.VMEM((SHIFT_SLOTS, SUBLANES - 1, COPY_ROWS, D_CONV), F32),
                        pltpu.VMEM((tm, D_MODEL), BF16),
                        pltpu.VMEM((tm, D_MODEL), BF16),
                        pltpu.VMEM((tm, D_FF), BF16)],
        compiler_params=pltpu.CompilerParams(
            dimension_semantics=("arbitrary",), vmem_limit_bytes=VMEM_LIMIT),
        name=f"stage_b_{'lat' if is_lat else 'ctx'}_{layer}",
    )(*args)


def _rope_tables(seq_len):
    t = jnp.arange(seq_len)
    r = (t // GRID_W).astype(F32)
    col = (t % GRID_W).astype(F32)
    nf = HEAD_D // 4
    inv = ROPE_BASE ** (-jnp.arange(nf, dtype=F32) / nf)
    ang_row = r[:, None] * inv
    ang_col = col[:, None] * inv
    cos = jnp.concatenate([jnp.cos(ang_row)] * 2 + [jnp.cos(ang_col)] * 2, axis=1)
    sin = jnp.concatenate([-jnp.sin(ang_row), jnp.sin(ang_row), -jnp.sin(ang_col), jnp.sin(ang_col)],
                          axis=1)
    return cos, sin


def kernel(x_prompt, x_sample, state_ret_fwd, state_ret_bwd, c, c_ctx, w_ada, b_ada, ln_g, ln_b, ffn_w1,
           ffn_w2, w_in, w_out, conv_w, conv_b, conv_ln_g, conv_ln_b, ret_decay_logit):
    batch, seq, _ = x_prompt.shape
    dec_batch, dec_seq, _ = x_sample.shape
    depth = w_ada.shape[0]
    alpha = (2.0 * depth) ** 0.25
    assert dec_batch + 1 <= COND_ROWS and seq % CHUNK == 0 and dec_seq % CHUNK == 0

    cond_all = jnp.zeros((COND_ROWS, D_MODEL), F32).at[0].set(c_ctx).at[1:1 + dec_batch].set(c)
    mod = _modulation(cond_all, w_ada, b_ada).reshape(depth, COND_ROWS, 1, N_SUB * 3 * D_MODEL)
    lg = jax.nn.log_sigmoid(ret_decay_logit.astype(F32)).reshape(-1)
    rope_tabs = _rope_tables(dec_seq)
    w1b = ffn_w1.astype(BF16)
    w2b = ffn_w2.astype(BF16)
    winb = w_in.astype(BF16)
    woutb = w_out.astype(BF16)
    conv_b3 = conv_b.reshape(depth, 1, D_CONV)
    clg3 = conv_ln_g.reshape(depth, 1, D_CONV)
    clb3 = conv_ln_b.reshape(depth, 1, D_CONV)

    tm_ctx = min(512, batch * seq)
    tm_lat = min(512, dec_seq)
    ctx = x_prompt.reshape(batch * seq, D_MODEL)
    lat = x_sample.reshape(dec_batch * dec_seq, D_MODEL)
    new_fwd = None
    new_bwd = None
    for l in range(depth):
        x1, mid, rb, new_bwd = _stage_a(l, False, alpha, ctx, seq, tm_ctx, lg, mod, ln_g, ln_b, w1b,
                                        w2b, winb, None, None, acc=new_bwd)
        ctx, new_fwd = _stage_b(l, False, alpha, x1, mid, rb, seq, tm_ctx, lg, mod, ln_g, ln_b, conv_w,
                                conv_b3, clg3, clb3, woutb, w1b, w2b, None, acc=new_fwd)
        x1, mid, rb = _stage_a(l, True, alpha, lat, dec_seq, tm_lat, lg, mod, ln_g, ln_b, w1b, w2b,
                               winb, rope_tabs, state_ret_bwd)
        (lat,) = _stage_b(l, True, alpha, x1, mid, rb, dec_seq, tm_lat, lg, mod, ln_g, ln_b, conv_w,
                          conv_b3, clg3, clb3, woutb, w1b, w2b, state_ret_fwd)
    return (ctx.reshape(batch, seq, D_MODEL), lat.reshape(dec_batch, dec_seq, D_MODEL), new_fwd, new_bwd)
```

```python
import functools

import jax
import jax.numpy as jnp
from jax import lax
from jax.experimental import pallas as pl
from jax.experimental.pallas import tpu as pltpu

D_MODEL = 1024
D_CONV = 512
D_RET = 512
N_HEADS = 4
HEAD_D = 128
CONV_K = 31
CONV_PAD = 15
CHUNK = 128
GRID_W = 64
D_FF = 2816
N_SUB = 3
D_IN = 2 * D_CONV + 4 * D_RET
D_MID = D_CONV + 4 * D_RET
ROPE_BASE = 10000.0
EPS = 1e-5
SUBLANES = 8
TOKEN_ROWS = 16
COND_ROWS = 16
CONV_ROWS = 32
CONV_WIN = 64
COPY_ROWS = CONV_WIN + (CONV_K // SUBLANES) * SUBLANES
SHIFT_SLOTS = 2
HALO = 16
FF_CHUNK = 256
N_FF_CHUNKS = D_FF // FF_CHUNK
VMEM_LIMIT = 56 * 1024 * 1024

F32 = jnp.float32
BF16 = jnp.bfloat16


def _dot(a, b):
    return jnp.dot(a, b, preferred_element_type=F32)


def _sigmoid(x):
    return 1.0 / (1.0 + jnp.exp(-x))


def _layer_norm(x, g=None, b=None):
    mu = jnp.mean(x, axis=-1, keepdims=True)
    xc = x - mu
    var = jnp.mean(xc * xc, axis=-1, keepdims=True)
    y = xc * lax.rsqrt(var + EPS)
    if g is not None:
        y = y * g + b
    return y


def _ffn_hidden_chunk(c, xm_ref, w1_ref, h_ref, rows=slice(None)):
    lo = c * FF_CHUNK
    a = _dot(xm_ref[rows, :], w1_ref[:, lo:lo + FF_CHUNK])
    b = _dot(xm_ref[rows, :], w1_ref[:, D_FF + lo:D_FF + lo + FF_CHUNK])
    h_ref[:, lo:lo + FF_CHUNK] = (a * _sigmoid(a) * b).astype(BF16)


def _ffn(xm_ref, w1_ref, w2_ref, h_ref):
    for c in range(N_FF_CHUNKS):
        _ffn_hidden_chunk(c, xm_ref, w1_ref, h_ref)
    return _dot(h_ref[...], w2_ref[...])


def _split_units(costs, work):
    total_cost, total_work = sum(costs), sum(work)
    starts, k, spent, done = [0], 0, 0, 0
    for w in work[:-1]:
        done += w
        while k < len(costs) and (spent + costs[k] / 2) * total_work <= done * total_cost:
            spent += costs[k]
            k += 1
        starts.append(k)
    starts.append(len(costs))
    return starts


def _mod_piece(mod_ref, i):
    return mod_ref[:, i * D_MODEL:(i + 1) * D_MODEL]


def _decay_rows(lg, offset, sign):
    i = lax.broadcasted_iota(jnp.int32, (CHUNK, HEAD_D), 0).astype(F32)
    return jnp.exp(lg * (offset + sign * i))


def _mod_kernel(cond_ref, w_ref, b_ref, out_ref):
    cnd = cond_ref[...]
    s = (cnd * _sigmoid(cnd)).astype(BF16)
    out_ref[...] = _dot(s, w_ref[...].astype(BF16)) + b_ref[...]


def _modulation(cond_all, w_ada, b_ada):
    depth = w_ada.shape[0]
    n_out = w_ada.shape[2]
    tn = n_out // 4
    return pl.pallas_call(
        _mod_kernel,
        grid=(depth, n_out // tn),
        in_specs=[
            pl.BlockSpec((COND_ROWS, D_MODEL), lambda l, j: (0, 0)),
            pl.BlockSpec((None, D_MODEL, tn), lambda l, j: (l, 0, j)),
            pl.BlockSpec((None, 1, tn), lambda l, j: (l, 0, j)),
        ],
        out_specs=pl.BlockSpec((None, COND_ROWS, tn), lambda l, j: (l, 0, j)),
        out_shape=jax.ShapeDtypeStruct((depth, COND_ROWS, n_out), F32),
        compiler_params=pltpu.CompilerParams(
            dimension_semantics=("arbitrary", "arbitrary"), vmem_limit_bytes=VMEM_LIMIT),
        name="adaln_modulation",
    )(cond_all, w_ada, b_ada.reshape(depth, 1, n_out))


def _drop_ref(refs, pos):
    return refs if pos is None else refs[:pos] + refs[pos + 1:]


def _stage_a_kernel(layer, is_lat, tm, tiles_per_seq, chunks_per_seq, alpha, acc_pos, *refs):
    refs = _drop_ref(refs, acc_pos)
    if is_lat:
        (lg_ref, x_ref, mod_ref, lng_ref, lnb_ref, w1_ref, w2_ref, win_ref, cos_ref, sin_ref, r0_ref,
         x1_ref, mid_ref, rb_ref, state_ref, kdec_ref, xm_ref, h_ref) = refs
    else:
        (lg_ref, x_ref, mod_ref, lng_ref, lnb_ref, w1_ref, w2_ref, win_ref,
         x1_ref, mid_ref, rb_ref, rfin_ref, state_ref, kdec_ref, xm_ref, h_ref) = refs
    step = pl.program_id(0)

    @pl.when(step == 0)
    def _():
        for h in range(N_HEADS):
            lg = lg_ref[layer * 2 * N_HEADS + N_HEADS + h]
            kdec_ref[h] = _decay_rows(lg, 0.0, 1.0)
            kdec_ref[N_HEADS + h] = _decay_rows(lg, float(CHUNK), 0.0)

    if is_lat:
        @pl.when(step % tiles_per_seq == 0)
        def _():
            state_ref[...] = r0_ref[...]

    xm_ref[...] = (x_ref[...] * (1.0 + _mod_piece(mod_ref, 1)) + _mod_piece(mod_ref, 0)).astype(BF16)
    y = _ffn(xm_ref, w1_ref, w2_ref, h_ref)
    x1 = _layer_norm(alpha * x_ref[...] + 0.5 * _mod_piece(mod_ref, 2) * y,
                     lng_ref[0:1, :], lnb_ref[0:1, :])
    x1_ref[...] = x1

    xm_ref[...] = (x1 * (1.0 + _mod_piece(mod_ref, 4)) + _mod_piece(mod_ref, 3)).astype(BF16)
    p = _dot(xm_ref[...], win_ref[...])
    k = p[:, 0:D_RET]
    v = p[:, D_RET:2 * D_RET]
    q = p[:, 2 * D_RET:3 * D_RET]
    o = 3 * D_RET
    ca = p[:, o:o + D_CONV]
    cg = p[:, o + D_CONV:o + 2 * D_CONV]
    g = p[:, o + 2 * D_CONV:o + 2 * D_CONV + D_RET]
    mid_ref[:, 0:D_CONV] = (ca * _sigmoid(cg)).astype(BF16)
    if is_lat:
        lane = lax.broadcasted_iota(jnp.int32, (tm, D_RET), 1)
        first_half = (lane % (HEAD_D // 2)) < (HEAD_D // 4)
        cos = jnp.concatenate([cos_ref[...]] * N_HEADS, axis=1)
        sin = jnp.concatenate([sin_ref[...]] * N_HEADS, axis=1)

        def rope(t):
            swapped = jnp.where(first_half, pltpu.roll(t, D_RET - HEAD_D // 4, 1),
                                pltpu.roll(t, HEAD_D // 4, 1))
            return t * cos + swapped * sin

        q = rope(q)
        k = rope(k)
    q = q * (HEAD_D ** -0.5)
    mid_ref[:, D_CONV:D_CONV + D_RET] = q.astype(BF16)
    mid_ref[:, D_CONV + D_RET:D_CONV + 2 * D_RET] = k.astype(BF16)
    vb = v.astype(BF16)
    mid_ref[:, D_CONV + 2 * D_RET:D_CONV + 3 * D_RET] = vb
    mid_ref[:, D_CONV + 3 * D_RET:D_CONV + 4 * D_RET] = (g * _sigmoid(g)).astype(BF16)

    n_chunks = tm // CHUNK
    for h in range(N_HEADS):
        cols = slice(h * HEAD_D, (h + 1) * HEAD_D)
        r = state_ref[h] if is_lat else None
        for ci in reversed(range(n_chunks)):
            rows = slice(ci * CHUNK, (ci + 1) * CHUNK)
            if not is_lat and (ci + 1) % chunks_per_seq == 0:
                r = jnp.zeros((HEAD_D, HEAD_D), F32)
            rb_ref[ci, h] = r
            kd = (k[rows, cols] * kdec_ref[h]).astype(BF16)
            kv = lax.dot_general(kd, vb[rows, cols], (((0,), (0,)), ((), ())),
                                 preferred_element_type=F32)
            r = kdec_ref[N_HEADS + h] * r + kv
            if not is_lat and ci % chunks_per_seq == 0:
                rfin_ref[ci // chunks_per_seq, h] = r
        if is_lat:
            state_ref[h] = r


def _const_spec(shape, index):
    return pl.BlockSpec(shape, lambda i: index, pipeline_mode=pl.Buffered(1))


def _stage_a(layer, is_lat, alpha, x2d, seq_len, tm, lg, mod, ln_g, ln_b, w1, w2, w_in, rope_tabs, r0,
             acc=None):
    depth = w1.shape[0]
    acc_pos, aliases = None, {}
    n_tok = x2d.shape[0]
    nt = n_tok // tm
    tiles_per_seq = max(seq_len // tm, 1)
    n_chunks = tm // CHUNK
    n_seq = n_tok // seq_len
    rev = lambda i: nt - 1 - i

    if is_lat:
        group = lambda i: 1 + rev(i) // tiles_per_seq
    else:
        group = lambda i: 0
    in_specs = [
        pl.BlockSpec(memory_space=pltpu.SMEM),
        pl.BlockSpec((tm, D_MODEL), lambda i: (rev(i), 0)),
        pl.BlockSpec((None, None, 1, N_SUB * 3 * D_MODEL), lambda i: (layer, group(i), 0, 0)),
        _const_spec((None, N_SUB, D_MODEL), (layer, 0, 0)),
        _const_spec((None, N_SUB, D_MODEL), (layer, 0, 0)),
        _const_spec((None, None, D_MODEL, 2 * D_FF), (layer, 0, 0, 0)),
        _const_spec((None, None, D_FF, D_MODEL), (layer, 0, 0, 0)),
        _const_spec((None, D_MODEL, D_IN), (layer, 0, 0)),
    ]
    args = [lg, x2d, mod, ln_g, ln_b, w1, w2, w_in]
    out_shape = [
        jax.ShapeDtypeStruct((n_tok, D_MODEL), F32),
        jax.ShapeDtypeStruct((n_tok, D_MID), BF16),
        jax.ShapeDtypeStruct((n_tok // CHUNK, N_HEADS, HEAD_D, HEAD_D), F32),
    ]
    out_specs = [
        pl.BlockSpec((tm, D_MODEL), lambda i: (rev(i), 0)),
        pl.BlockSpec((tm, D_MID), lambda i: (rev(i), 0)),
        pl.BlockSpec((n_chunks, N_HEADS, HEAD_D, HEAD_D), lambda i: (rev(i), 0, 0, 0)),
    ]
    if is_lat:
        cos_t, sin_t = rope_tabs
        in_specs += [
            pl.BlockSpec((tm, HEAD_D), lambda i: (rev(i) % tiles_per_seq, 0)),
            pl.BlockSpec((tm, HEAD_D), lambda i: (rev(i) % tiles_per_seq, 0)),
            pl.BlockSpec((None, None, N_HEADS, HEAD_D, HEAD_D),
                         lambda i: (rev(i) // tiles_per_seq, layer, 0, 0, 0)),
        ]
        args += [cos_t, sin_t, r0]
    else:
        seqs_per_tile = tm // seq_len
        out_shape.append(jax.ShapeDtypeStruct((n_seq, depth, N_HEADS, HEAD_D, HEAD_D), F32))
        out_specs.append(pl.BlockSpec((seqs_per_tile, None, N_HEADS, HEAD_D, HEAD_D),
                                      lambda i: (rev(i), layer, 0, 0, 0)))
        if acc is not None:
            acc_pos = len(args)
            aliases = {acc_pos: len(out_shape) - 1}
            in_specs.append(pl.BlockSpec(memory_space=pl.ANY))
            args.append(acc)
    return pl.pallas_call(
        functools.partial(_stage_a_kernel, layer, is_lat, tm, tiles_per_seq,
                          min(seq_len, tm) // CHUNK, alpha, acc_pos),
        grid=(nt,),
        in_specs=in_specs,
        out_specs=out_specs,
        out_shape=out_shape,
        input_output_aliases=aliases,
        scratch_shapes=[pltpu.VMEM((N_HEADS, HEAD_D, HEAD_D), F32),
                        pltpu.VMEM((2 * N_HEADS, CHUNK, HEAD_D), F32),
                        pltpu.VMEM((tm, D_MODEL), BF16),
                        pltpu.VMEM((tm, D_FF), BF16)],
        compiler_params=pltpu.CompilerParams(
            dimension_semantics=("arbitrary",), vmem_limit_bytes=VMEM_LIMIT),
        name=f"stage_a_{'lat' if is_lat else 'ctx'}_{layer}",
    )(*args)


def _stage_b_kernel(layer, is_lat, tm, nt, tiles_per_seq, chunks_per_seq, conv_dom, alpha, acc_pos,
                    *refs):
    refs = _drop_ref(refs, acc_pos)
    if is_lat:
        (lg_ref, zero_ref, x1_ref, mid_ref, rb_ref, mod_ref, lng_ref, lnb_ref, cw_ref, cb_ref, clg_ref,
         clb_ref, wout_ref, w1_ref, w2_ref, r0_ref,
         out_ref, state_ref, tab_ref, upad_ref, shift_ref, mix_ref, xm_ref, h_ref) = refs
    else:
        (lg_ref, zero_ref, x1_ref, mid_ref, rb_ref, mod_ref, lng_ref, lnb_ref, cw_ref, cb_ref, clg_ref,
         clb_ref, wout_ref, w1_ref, w2_ref,
         out_ref, rfin_ref, state_ref, tab_ref, upad_ref, shift_ref, mix_ref, xm_ref, h_ref) = refs
    step = pl.program_id(0)
    n_dom = tm // conv_dom
    dom_rows = conv_dom + 2 * HALO
    copy_rows = COPY_ROWS
    n_chunks = tm // CHUNK

    @pl.when(step == 0)
    def _():
        ii = lax.broadcasted_iota(jnp.int32, (CHUNK, CHUNK), 0)
        jj = lax.broadcasted_iota(jnp.int32, (CHUNK, CHUNK), 1)
        dist = (ii - jj).astype(F32)
        for h in range(N_HEADS):
            lgf = lg_ref[layer * 2 * N_HEADS + h]
            lgb = lg_ref[layer * 2 * N_HEADS + N_HEADS + h]
            fwd = jnp.where(dist >= 0, jnp.exp(lgf * jnp.maximum(dist, 0.0)), 0.0)
            bwd = jnp.where(dist <= 0, jnp.exp(lgb * jnp.maximum(-dist, 0.0)), 0.0)
            tab_ref[h, 0] = fwd + bwd
            tab_ref[h, 1] = _decay_rows(lgf, 1.0, 1.0)
            tab_ref[h, 2] = _decay_rows(lgb, float(CHUNK), -1.0)
            tab_ref[h, 3] = _decay_rows(lgf, CHUNK - 1.0, -1.0)
            tab_ref[h, 4] = _decay_rows(lgf, float(CHUNK), 0.0)
        zeros = jnp.zeros((HALO, D_CONV), F32)
        for j in range(n_dom):
            upad_ref[j * dom_rows:j * dom_rows + HALO, :] = zeros
            upad_ref[j * dom_rows + HALO + conv_dom:(j + 1) * dom_rows, :] = zeros
        mix_ref[...] = jnp.zeros((tm, D_MODEL), BF16)

    if is_lat:
        @pl.when(jnp.minimum(step, nt - 1) % tiles_per_seq == 0)
        def _():
            state_ref[...] = r0_ref[...]

    units = []
    unit_cost = []

    def conv_fill(j, w, slot):
        def run():
            base = j * dom_rows
            if w == 0:
                upad_ref[base + HALO:base + HALO + conv_dom, :] = (
                    mid_ref[j * conv_dom:(j + 1) * conv_dom, 0:D_CONV].astype(F32))
            lo = base + w * CONV_WIN
            for b in range(1, SUBLANES):
                shift_ref[slot, b - 1] = upad_ref[lo + b:lo + b + copy_rows, :]
        return run

    def conv_block(j, w, r0, slot):
        def run():
            acc = jnp.broadcast_to(cb_ref[...], (CONV_ROWS, D_CONV))
            for d in range(CONV_K):
                off = HALO - CONV_PAD + d
                a, b = off // SUBLANES, off % SUBLANES
                if b == 0:
                    lo = j * dom_rows + w * CONV_WIN + a * SUBLANES + r0
                    tap = upad_ref[lo:lo + CONV_ROWS, :]
                else:
                    lo = a * SUBLANES + r0
                    tap = shift_ref[slot, b - 1, lo:lo + CONV_ROWS, :]
                acc = acc + cw_ref[d:d + 1, :] * tap
            uc = _layer_norm(acc, clg_ref[...], clb_ref[...])
            first = j * conv_dom + w * CONV_WIN + r0
            mix_ref[first:first + CONV_ROWS, 0:D_CONV] = (uc * _sigmoid(uc)).astype(BF16)
            return uc[0:TOKEN_ROWS, 0:HEAD_D]
        return run

    conv_units = []
    for j in range(n_dom):
        for w in range(conv_dom // CONV_WIN):
            slot = (j * (conv_dom // CONV_WIN) + w) % SHIFT_SLOTS
            fill = conv_fill(j, w, slot)
            for r0 in range(0, CONV_WIN, CONV_ROWS):
                conv_units.append((fill, conv_block(j, w, r0, slot)))
                fill = None

    def ret_unit(h, ci):
        qc = slice(D_CONV + h * HEAD_D, D_CONV + (h + 1) * HEAD_D)
        kc = slice(D_CONV + D_RET + h * HEAD_D, D_CONV + D_RET + (h + 1) * HEAD_D)
        vc = slice(D_CONV + 2 * D_RET + h * HEAD_D, D_CONV + 2 * D_RET + (h + 1) * HEAD_D)
        gc = slice(D_CONV + 3 * D_RET + h * HEAD_D, D_CONV + 3 * D_RET + (h + 1) * HEAD_D)

        def run():
            rows = slice(ci * CHUNK, (ci + 1) * CHUNK)
            if ci % chunks_per_seq == 0 and not is_lat:
                r = jnp.zeros((HEAD_D, HEAD_D), F32)
            else:
                r = state_ref[h]
            qh = mid_ref[rows, qc]
            kh = mid_ref[rows, kc]
            vh = mid_ref[rows, vc]
            s = lax.dot_general(qh, kh, (((1,), (1,)), ((), ())), preferred_element_type=F32)
            o = _dot((s * tab_ref[h, 0]).astype(BF16), vh)
            qf = qh.astype(F32)
            qd = jnp.concatenate([(qf * tab_ref[h, 1]).astype(BF16),
                                  (qf * tab_ref[h, 2]).astype(BF16)], axis=1)
            rcat = jnp.concatenate([r.astype(BF16), rb_ref[ci, h].astype(BF16)], axis=0)
            o = o + _dot(qd, rcat)
            kd = (kh.astype(F32) * tab_ref[h, 3]).astype(BF16)
            kv = lax.dot_general(kd, vh, (((0,), (0,)), ((), ())), preferred_element_type=F32)
            r = tab_ref[h, 4] * r + kv
            state_ref[h] = r
            if (ci + 1) % chunks_per_seq == 0 and not is_lat:
                rfin_ref[ci // chunks_per_seq, h] = r
            on = _layer_norm(o)
            mix_ref[rows, D_CONV + h * HEAD_D:D_CONV + (h + 1) * HEAD_D] = (
                on * mid_ref[rows, gc].astype(F32)).astype(BF16)
            return on[0:TOKEN_ROWS, :]
        return run

    ret_units = [ret_unit(h, ci) for ci in range(n_chunks) for h in range(N_HEADS)]

    for k in range(max(len(conv_units), len(ret_units))):
        if k < len(conv_units):
            fill, block = conv_units[k]
            if fill is not None:
                units.append(fill)
                unit_cost.append(3 * SUBLANES * CONV_WIN // CONV_ROWS)
            units.append(block)
            unit_cost.append(2 * CONV_K + 2 * SUBLANES)
        if k < len(ret_units):
            units.append(ret_units[k])
            unit_cost.append(3 * SUBLANES)

    n_phases = N_FF_CHUNKS + 2
    chunk_work = D_MODEL * 2 * FF_CHUNK
    phase_work = [D_MODEL * D_MODEL] + [chunk_work] * N_FF_CHUNKS + [chunk_work]
    starts = _split_units(unit_cost, phase_work)

    anchored = zero_ref[0] != 0

    def run_units(p):
        tok = None
        for u in units[starts[p]:starts[p + 1]]:
            piece = u()
            if piece is not None:
                piece = jnp.where(anchored, piece, 0.0)
                tok = piece if tok is None else tok + piece
        return tok

    def fold_into(ref, cols, tok):
        if tok is not None:
            ref[0:TOKEN_ROWS, cols] = (ref[0:TOKEN_ROWS, cols].astype(F32) + tok).astype(BF16)

    y = _dot(mix_ref[...], wout_ref[...])
    x2 = _layer_norm(alpha * x1_ref[...] + _mod_piece(mod_ref, 5) * y, lng_ref[1:2, :], lnb_ref[1:2, :])
    out_ref[...] = x2
    xm_ref[...] = (x2 * (1.0 + _mod_piece(mod_ref, 7)) + _mod_piece(mod_ref, 6)).astype(BF16)
    first = run_units(0)
    for c in range(N_FF_CHUNKS):
        _ffn_hidden_chunk(c, xm_ref, w1_ref, h_ref)
        if c == 0:
            fold_into(h_ref, slice(0, HEAD_D), first)
        fold_into(h_ref, slice(c * FF_CHUNK, c * FF_CHUNK + HEAD_D), run_units(c + 1))
    y2 = _dot(h_ref[...], w2_ref[...])
    tail = run_units(n_phases - 1)
    out = _layer_norm(alpha * out_ref[...] + 0.5 * _mod_piece(mod_ref, 8) * y2,
                      lng_ref[2:3, :], lnb_ref[2:3, :])
    out_ref[...] = out
    if tail is not None:
        out_ref[0:TOKEN_ROWS, 0:HEAD_D] = out[0:TOKEN_ROWS, 0:HEAD_D] + tail


def _stage_b(layer, is_lat, alpha, x1, mid, rb, seq_len, tm, lg, mod, ln_g, ln_b, conv_w, conv_b, conv_ln_g,
             conv_ln_b, w_out, w1, w2, r0, acc=None):
    depth = w1.shape[0]
    acc_pos, aliases = None, {}
    n_tok = x1.shape[0]
    nt = n_tok // tm
    tiles_per_seq = max(seq_len // tm, 1)
    n_chunks = tm // CHUNK
    n_seq = n_tok // seq_len
    conv_dom = GRID_W if is_lat else seq_len
    n_dom = tm // conv_dom

    ftile = lambda i: jnp.maximum(i - 1, 0)
    mtile = lambda i: jnp.minimum(i, nt - 1)
    if is_lat:
        group = lambda i: 1 + ftile(i) // tiles_per_seq
    else:
        group = lambda i: 0
    in_specs = [
        pl.BlockSpec(memory_space=pltpu.SMEM),
        pl.BlockSpec(memory_space=pltpu.SMEM),
        pl.BlockSpec((tm, D_MODEL), lambda i: (ftile(i), 0)),
        pl.BlockSpec((tm, D_MID), lambda i: (mtile(i), 0)),
        pl.BlockSpec((n_chunks, N_HEADS, HEAD_D, HEAD_D), lambda i: (mtile(i), 0, 0, 0)),
        pl.BlockSpec((None, None, 1, N_SUB * 3 * D_MODEL), lambda i: (layer, group(i), 0, 0)),
        _const_spec((None, N_SUB, D_MODEL), (layer, 0, 0)),
        _const_spec((None, N_SUB, D_MODEL), (layer, 0, 0)),
        _const_spec((None, CONV_K, D_CONV), (layer, 0, 0)),
        _const_spec((None, 1, D_CONV), (layer, 0, 0)),
        _const_spec((None, 1, D_CONV), (layer, 0, 0)),
        _const_spec((None, 1, D_CONV), (layer, 0, 0)),
        _const_spec((None, D_MODEL, D_MODEL), (layer, 0, 0)),
        _const_spec((None, None, D_MODEL, 2 * D_FF), (layer, 1, 0, 0)),
        _const_spec((None, None, D_FF, D_MODEL), (layer, 1, 0, 0)),
    ]
    args = [lg, jnp.zeros((1,), jnp.int32), x1, mid, rb, mod, ln_g, ln_b, conv_w, conv_b, conv_ln_g,
            conv_ln_b, w_out, w1, w2]
    out_shape = [jax.ShapeDtypeStruct((n_tok, D_MODEL), F32)]
    out_specs = [pl.BlockSpec((tm, D_MODEL), lambda i: (ftile(i), 0))]
    if is_lat:
        in_specs.append(pl.BlockSpec((None, None, N_HEADS, HEAD_D, HEAD_D),
                                     lambda i: (mtile(i) // tiles_per_seq, layer, 0, 0, 0)))
        args.append(r0)
    else:
        seqs_per_tile = tm // seq_len
        out_shape.append(jax.ShapeDtypeStruct((n_seq, depth, N_HEADS, HEAD_D, HEAD_D), F32))
        out_specs.append(pl.BlockSpec((seqs_per_tile, None, N_HEADS, HEAD_D, HEAD_D),
                                      lambda i: (mtile(i), layer, 0, 0, 0)))
        if acc is not None:
            acc_pos = len(args)
            aliases = {acc_pos: len(out_shape) - 1}
            in_specs.append(pl.BlockSpec(memory_space=pl.ANY))
            args.append(acc)
    return pl.pallas_call(
        functools.partial(_stage_b_kernel, layer, is_lat, tm, nt, tiles_per_seq,
                          min(seq_len, tm) // CHUNK, conv_dom, alpha, acc_pos),
        grid=(nt + 1,),
        in_specs=in_specs,
        out_specs=out_specs,
        out_shape=out_shape,
        input_output_aliases=aliases,
        scratch_shapes=[pltpu.VMEM((N_HEADS, HEAD_D, HEAD_D), F32),
                        pltpu.VMEM((N_HEADS, 5, CHUNK, CHUNK), F32),
                        pltpu.VMEM((n_dom * (conv_dom + 2 * HALO), D_CONV), F32),
                        pltpu.VMEM((SHIFT_SLOTS, SUBLANES - 1, COPY_ROWS, D_CONV), F32),
                        pltpu.VMEM((tm, D_MODEL), BF16),
                        pltpu.VMEM((tm, D_MODEL), BF16),
                        pltpu.VMEM((tm, D_FF), BF16)],
        compiler_params=pltpu.CompilerParams(
            dimension_semantics=("arbitrary",), vmem_limit_bytes=VMEM_LIMIT),
        name=f"stage_b_{'lat' if is_lat else 'ctx'}_{layer}",
    )(*args)


def _rope_tables(seq_len):
    t = jnp.arange(seq_len)
    r = (t // GRID_W).astype(F32)
    col = (t % GRID_W).astype(F32)
    nf = HEAD_D // 4
    inv = ROPE_BASE ** (-jnp.arange(nf, dtype=F32) / nf)
    ang_row = r[:, None] * inv
    ang_col = col[:, None] * inv
    cos = jnp.concatenate([jnp.cos(ang_row)] * 2 + [jnp.cos(ang_col)] * 2, axis=1)
    sin = jnp.concatenate([-jnp.sin(ang_row), jnp.sin(ang_row), -jnp.sin(ang_col), jnp.sin(ang_col)],
                          axis=1)
    return cos, sin


def _reorder_w_in(w_in):
    c, r = 2 * D_CONV, D_RET
    return jnp.concatenate([w_in[..., c + r:c + 3 * r], w_in[..., c:c + r], w_in[..., 0:c],
                            w_in[..., c + 3 * r:]], axis=-1)


def kernel(x_prompt, x_sample, state_ret_fwd, state_ret_bwd, c, c_ctx, w_ada, b_ada, ln_g, ln_b, ffn_w1,
           ffn_w2, w_in, w_out, conv_w, conv_b, conv_ln_g, conv_ln_b, ret_decay_logit):
    batch, seq, _ = x_prompt.shape
    dec_batch, dec_seq, _ = x_sample.shape
    depth = w_ada.shape[0]
    alpha = (2.0 * depth) ** 0.25
    assert dec_batch + 1 <= COND_ROWS and seq % CHUNK == 0 and dec_seq % CHUNK == 0

    cond_all = jnp.zeros((COND_ROWS, D_MODEL), F32).at[0].set(c_ctx).at[1:1 + dec_batch].set(c)
    mod = _modulation(cond_all, w_ada, b_ada).reshape(depth, COND_ROWS, 1, N_SUB * 3 * D_MODEL)
    lg = jax.nn.log_sigmoid(ret_decay_logit.astype(F32)).reshape(-1)
    rope_tabs = _rope_tables(dec_seq)
    w1b = ffn_w1.astype(BF16)
    w2b = ffn_w2.astype(BF16)
    winb = _reorder_w_in(w_in).astype(BF16)
    woutb = w_out.astype(BF16)
    conv_b3 = conv_b.reshape(depth, 1, D_CONV)
    clg3 = conv_ln_g.reshape(depth, 1, D_CONV)
    clb3 = conv_ln_b.reshape(depth, 1, D_CONV)

    tm_ctx = min(512, batch * seq)
    tm_lat = min(512, dec_seq)
    ctx = x_prompt.reshape(batch * seq, D_MODEL)
    lat = x_sample.reshape(dec_batch * dec_seq, D_MODEL)
    new_fwd = jnp.zeros((batch, depth, N_HEADS, HEAD_D, HEAD_D), F32)
    new_bwd = jnp.zeros((batch, depth, N_HEADS, HEAD_D, HEAD_D), F32)
    for l in range(depth):
        x1, mid, rb, new_bwd = _stage_a(l, False, alpha, ctx, seq, tm_ctx, lg, mod, ln_g, ln_b, w1b,
                                        w2b, winb, None, None, acc=new_bwd)
        ctx, new_fwd = _stage_b(l, False, alpha, x1, mid, rb, seq, tm_ctx, lg, mod, ln_g, ln_b, conv_w,
                                conv_b3, clg3, clb3, woutb, w1b, w2b, None, acc=new_fwd)
        x1, mid, rb = _stage_a(l, True, alpha, lat, dec_seq, tm_lat, lg, mod, ln_g, ln_b, w1b, w2b,
                               winb, rope_tabs, state_ret_bwd)
        (lat,) = _stage_b(l, True, alpha, x1, mid, rb, dec_seq, tm_lat, lg, mod, ln_g, ln_b, conv_w,
                          conv_b3, clg3, clb3, woutb, w1b, w2b, state_ret_fwd)
    return (ctx.reshape(batch, seq, D_MODEL), lat.reshape(dec_batch, dec_seq, D_MODEL), new_fwd, new_bwd)
```

```python
import functools

import jax
import jax.numpy as jnp
from jax import lax
from jax.experimental import pallas as pl
from jax.experimental.pallas import tpu as pltpu

D_MODEL = 1024
D_CONV = 512
D_RET = 512
N_HEADS = 4
HEAD_D = 128
CONV_K = 31
CONV_PAD = 15
CHUNK = 128
GRID_W = 64
D_FF = 2816
N_SUB = 3
D_IN = 2 * D_CONV + 4 * D_RET
D_MID = D_CONV + 4 * D_RET
ROPE_BASE = 10000.0
EPS = 1e-5
SUBLANES = 8
TOKEN_ROWS = 16
COND_ROWS = 16
CONV_ROWS = 32
CONV_WIN = 64
COPY_ROWS = CONV_WIN + (CONV_K // SUBLANES) * SUBLANES
SHIFT_SLOTS = 2
HALO = 16
FF_CHUNK = 256
N_FF_CHUNKS = D_FF // FF_CHUNK
VMEM_LIMIT = 56 * 1024 * 1024

F32 = jnp.float32
BF16 = jnp.bfloat16


def _dot(a, b):
    return jnp.dot(a, b, preferred_element_type=F32)


NEG_LOG2_E = -1.4426950408889634


def _sigmoid(x):
    return 1.0 / (1.0 + jnp.exp2(x * NEG_LOG2_E))


def _layer_norm(x, g=None, b=None):
    mu = jnp.mean(x, axis=-1, keepdims=True)
    xc = x - mu
    var = jnp.mean(xc * xc, axis=-1, keepdims=True)
    y = xc * lax.rsqrt(var + EPS)
    if g is not None:
        y = y * g + b
    return y


def _ffn_hidden_chunk(c, xm_ref, w1_ref, h_ref, rows=slice(None)):
    lo = c * FF_CHUNK
    a = _dot(xm_ref[rows, :], w1_ref[:, lo:lo + FF_CHUNK])
    b = _dot(xm_ref[rows, :], w1_ref[:, D_FF + lo:D_FF + lo + FF_CHUNK])
    h_ref[:, lo:lo + FF_CHUNK] = (a * _sigmoid(a) * b).astype(BF16)


def _ffn(xm_ref, w1_ref, w2_ref, h_ref):
    for c in range(N_FF_CHUNKS):
        _ffn_hidden_chunk(c, xm_ref, w1_ref, h_ref)
    return _dot(h_ref[...], w2_ref[...])


def _split_units(costs, work):
    total_cost, total_work = sum(costs), sum(work)
    starts, k, spent, done = [0], 0, 0, 0
    for w in work[:-1]:
        done += w
        while k < len(costs) and (spent + costs[k] / 2) * total_work <= done * total_cost:
            spent += costs[k]
            k += 1
        starts.append(k)
    starts.append(len(costs))
    return starts


def _mod_piece(mod_ref, i):
    return mod_ref[:, i * D_MODEL:(i + 1) * D_MODEL]


def _decay_rows(lg, offset, sign):
    i = lax.broadcasted_iota(jnp.int32, (CHUNK, HEAD_D), 0).astype(F32)
    return jnp.exp(lg * (offset + sign * i))


def _mod_kernel(cond_ref, w_ref, b_ref, out_ref):
    cnd = cond_ref[...]
    s = (cnd * _sigmoid(cnd)).astype(BF16)
    out_ref[...] = _dot(s, w_ref[...].astype(BF16)) + b_ref[...]


def _modulation(cond_all, w_ada, b_ada):
    depth = w_ada.shape[0]
    n_out = w_ada.shape[2]
    tn = n_out // 4
    return pl.pallas_call(
        _mod_kernel,
        grid=(depth, n_out // tn),
        in_specs=[
            pl.BlockSpec((COND_ROWS, D_MODEL), lambda l, j: (0, 0)),
            pl.BlockSpec((None, D_MODEL, tn), lambda l, j: (l, 0, j)),
            pl.BlockSpec((None, 1, tn), lambda l, j: (l, 0, j)),
        ],
        out_specs=pl.BlockSpec((None, COND_ROWS, tn), lambda l, j: (l, 0, j)),
        out_shape=jax.ShapeDtypeStruct((depth, COND_ROWS, n_out), F32),
        compiler_params=pltpu.CompilerParams(
            dimension_semantics=("arbitrary", "arbitrary"), vmem_limit_bytes=VMEM_LIMIT),
        name="adaln_modulation",
    )(cond_all, w_ada, b_ada.reshape(depth, 1, n_out))


def _drop_ref(refs, pos):
    return refs if pos is None else refs[:pos] + refs[pos + 1:]


def _stage_a_kernel(layer, is_lat, tm, tiles_per_seq, chunks_per_seq, alpha, acc_pos, *refs):
    refs = _drop_ref(refs, acc_pos)
    if is_lat:
        (lg_ref, x_ref, mod_ref, lng_ref, lnb_ref, w1_ref, w2_ref, win_ref, cos_ref, sin_ref, r0_ref,
         x1_ref, mid_ref, rb_ref, state_ref, kdec_ref, xm_ref, h_ref) = refs
    else:
        (lg_ref, x_ref, mod_ref, lng_ref, lnb_ref, w1_ref, w2_ref, win_ref,
         x1_ref, mid_ref, rb_ref, rfin_ref, state_ref, kdec_ref, xm_ref, h_ref) = refs
    step = pl.program_id(0)

    @pl.when(step == 0)
    def _():
        for h in range(N_HEADS):
            lg = lg_ref[layer * 2 * N_HEADS + N_HEADS + h]
            kdec_ref[h] = _decay_rows(lg, 0.0, 1.0)
            kdec_ref[N_HEADS + h] = _decay_rows(lg, float(CHUNK), 0.0)

    if is_lat:
        @pl.when(step % tiles_per_seq == 0)
        def _():
            state_ref[...] = r0_ref[...]

    xm_ref[...] = (x_ref[...] * (1.0 + _mod_piece(mod_ref, 1)) + _mod_piece(mod_ref, 0)).astype(BF16)
    y = _ffn(xm_ref, w1_ref, w2_ref, h_ref)
    x1 = _layer_norm(alpha * x_ref[...] + 0.5 * _mod_piece(mod_ref, 2) * y,
                     lng_ref[0:1, :], lnb_ref[0:1, :])
    x1_ref[...] = x1

    xm_ref[...] = (x1 * (1.0 + _mod_piece(mod_ref, 4)) + _mod_piece(mod_ref, 3)).astype(BF16)
    p = _dot(xm_ref[...], win_ref[...])
    k = p[:, 0:D_RET]
    v = p[:, D_RET:2 * D_RET]
    q = p[:, 2 * D_RET:3 * D_RET]
    o = 3 * D_RET
    ca = p[:, o:o + D_CONV]
    cg = p[:, o + D_CONV:o + 2 * D_CONV]
    g = p[:, o + 2 * D_CONV:o + 2 * D_CONV + D_RET]
    mid_ref[:, 0:D_CONV] = (ca * _sigmoid(cg)).astype(BF16)
    if is_lat:
        lane = lax.broadcasted_iota(jnp.int32, (tm, D_RET), 1)
        first_half = (lane % (HEAD_D // 2)) < (HEAD_D // 4)
        cos = jnp.concatenate([cos_ref[...]] * N_HEADS, axis=1)
        sin = jnp.concatenate([sin_ref[...]] * N_HEADS, axis=1)

        def rope(t):
            swapped = jnp.where(first_half, pltpu.roll(t, D_RET - HEAD_D // 4, 1),
                                pltpu.roll(t, HEAD_D // 4, 1))
            return t * cos + swapped * sin

        q = rope(q)
        k = rope(k)
    q = q * (HEAD_D ** -0.5)
    mid_ref[:, D_CONV:D_CONV + D_RET] = q.astype(BF16)
    mid_ref[:, D_CONV + D_RET:D_CONV + 2 * D_RET] = k.astype(BF16)
    vb = v.astype(BF16)
    mid_ref[:, D_CONV + 2 * D_RET:D_CONV + 3 * D_RET] = vb
    mid_ref[:, D_CONV + 3 * D_RET:D_CONV + 4 * D_RET] = (g * _sigmoid(g)).astype(BF16)

    n_chunks = tm // CHUNK
    for h in range(N_HEADS):
        cols = slice(h * HEAD_D, (h + 1) * HEAD_D)
        r = state_ref[h] if is_lat else None
        for ci in reversed(range(n_chunks)):
            rows = slice(ci * CHUNK, (ci + 1) * CHUNK)
            if not is_lat and (ci + 1) % chunks_per_seq == 0:
                r = jnp.zeros((HEAD_D, HEAD_D), F32)
            rb_ref[ci, h] = r
            kd = (k[rows, cols] * kdec_ref[h]).astype(BF16)
            kv = lax.dot_general(kd, vb[rows, cols], (((0,), (0,)), ((), ())),
                                 preferred_element_type=F32)
            r = kdec_ref[N_HEADS + h] * r + kv
            if not is_lat and ci % chunks_per_seq == 0:
                rfin_ref[ci // chunks_per_seq, h] = r
        if is_lat:
            state_ref[h] = r


def _const_spec(shape, index):
    return pl.BlockSpec(shape, lambda i: index, pipeline_mode=pl.Buffered(1))


def _stage_a(layer, is_lat, alpha, x2d, seq_len, tm, lg, mod, ln_g, ln_b, w1, w2, w_in, rope_tabs, r0,
             acc=None):
    depth = w1.shape[0]
    acc_pos, aliases = None, {}
    n_tok = x2d.shape[0]
    nt = n_tok // tm
    tiles_per_seq = max(seq_len // tm, 1)
    n_chunks = tm // CHUNK
    n_seq = n_tok // seq_len
    rev = lambda i: nt - 1 - i

    if is_lat:
        group = lambda i: 1 + rev(i) // tiles_per_seq
    else:
        group = lambda i: 0
    in_specs = [
        pl.BlockSpec(memory_space=pltpu.SMEM),
        pl.BlockSpec((tm, D_MODEL), lambda i: (rev(i), 0)),
        pl.BlockSpec((None, None, 1, N_SUB * 3 * D_MODEL), lambda i: (layer, group(i), 0, 0)),
        _const_spec((None, N_SUB, D_MODEL), (layer, 0, 0)),
        _const_spec((None, N_SUB, D_MODEL), (layer, 0, 0)),
        _const_spec((None, None, D_MODEL, 2 * D_FF), (layer, 0, 0, 0)),
        _const_spec((None, None, D_FF, D_MODEL), (layer, 0, 0, 0)),
        _const_spec((None, D_MODEL, D_IN), (layer, 0, 0)),
    ]
    args = [lg, x2d, mod, ln_g, ln_b, w1, w2, w_in]
    out_shape = [
        jax.ShapeDtypeStruct((n_tok, D_MODEL), F32),
        jax.ShapeDtypeStruct((n_tok, D_MID), BF16),
        jax.ShapeDtypeStruct((n_tok // CHUNK, N_HEADS, HEAD_D, HEAD_D), F32),
    ]
    out_specs = [
        pl.BlockSpec((tm, D_MODEL), lambda i: (rev(i), 0)),
        pl.BlockSpec((tm, D_MID), lambda i: (rev(i), 0)),
        pl.BlockSpec((n_chunks, N_HEADS, HEAD_D, HEAD_D), lambda i: (rev(i), 0, 0, 0)),
    ]
    if is_lat:
        cos_t, sin_t = rope_tabs
        in_specs += [
            pl.BlockSpec((tm, HEAD_D), lambda i: (rev(i) % tiles_per_seq, 0)),
            pl.BlockSpec((tm, HEAD_D), lambda i: (rev(i) % tiles_per_seq, 0)),
            pl.BlockSpec((None, None, N_HEADS, HEAD_D, HEAD_D),
                         lambda i: (rev(i) // tiles_per_seq, layer, 0, 0, 0)),
        ]
        args += [cos_t, sin_t, r0]
    else:
        seqs_per_tile = tm // seq_len
        out_shape.append(jax.ShapeDtypeStruct((n_seq, depth, N_HEADS, HEAD_D, HEAD_D), F32))
        out_specs.append(pl.BlockSpec((seqs_per_tile, None, N_HEADS, HEAD_D, HEAD_D),
                                      lambda i: (rev(i), layer, 0, 0, 0)))
        if acc is not None:
            acc_pos = len(args)
            aliases = {acc_pos: len(out_shape) - 1}
            in_specs.append(pl.BlockSpec(memory_space=pl.ANY))
            args.append(acc)
    return pl.pallas_call(
        functools.partial(_stage_a_kernel, layer, is_lat, tm, tiles_per_seq,
                          min(seq_len, tm) // CHUNK, alpha, acc_pos),
        grid=(nt,),
        in_specs=in_specs,
        out_specs=out_specs,
        out_shape=out_shape,
        input_output_aliases=aliases,
        scratch_shapes=[pltpu.VMEM((N_HEADS, HEAD_D, HEAD_D), F32),
                        pltpu.VMEM((2 * N_HEADS, CHUNK, HEAD_D), F32),
                        pltpu.VMEM((tm, D_MODEL), BF16),
                        pltpu.VMEM((tm, D_FF), BF16)],
        compiler_params=pltpu.CompilerParams(
            dimension_semantics=("arbitrary",), vmem_limit_bytes=VMEM_LIMIT),
        name=f"stage_a_{'lat' if is_lat else 'ctx'}_{layer}",
    )(*args)


def _stage_b_kernel(layer, is_lat, tm, nt, tiles_per_seq, chunks_per_seq, conv_dom, alpha, acc_pos,
                    *refs):
    refs = _drop_ref(refs, acc_pos)
    if is_lat:
        (lg_ref, zero_ref, x1_ref, mid_ref, rb_ref, mod_ref, lng_ref, lnb_ref, cw_ref, cb_ref, clg_ref,
         clb_ref, wout_ref, w1_ref, w2_ref, r0_ref,
         out_ref, state_ref, tab_ref, upad_ref, shift_ref, mix_ref, xm_ref, h_ref, wtap_ref) = refs
    else:
        (lg_ref, zero_ref, x1_ref, mid_ref, rb_ref, mod_ref, lng_ref, lnb_ref, cw_ref, cb_ref, clg_ref,
         clb_ref, wout_ref, w1_ref, w2_ref,
         out_ref, rfin_ref, state_ref, tab_ref, upad_ref, shift_ref, mix_ref, xm_ref, h_ref,
         wtap_ref) = refs
    step = pl.program_id(0)
    n_dom = tm // conv_dom
    dom_rows = conv_dom + 2 * HALO
    copy_rows = COPY_ROWS
    n_chunks = tm // CHUNK

    @pl.when(step == 0)
    def _():
        ii = lax.broadcasted_iota(jnp.int32, (CHUNK, CHUNK), 0)
        jj = lax.broadcasted_iota(jnp.int32, (CHUNK, CHUNK), 1)
        dist = (ii - jj).astype(F32)
        for h in range(N_HEADS):
            lgf = lg_ref[layer * 2 * N_HEADS + h]
            lgb = lg_ref[layer * 2 * N_HEADS + N_HEADS + h]
            fwd = jnp.where(dist >= 0, jnp.exp(lgf * jnp.maximum(dist, 0.0)), 0.0)
            bwd = jnp.where(dist <= 0, jnp.exp(lgb * jnp.maximum(-dist, 0.0)), 0.0)
            tab_ref[h, 0] = fwd + bwd
            tab_ref[h, 1] = _decay_rows(lgf, 1.0, 1.0)
            tab_ref[h, 2] = _decay_rows(lgb, float(CHUNK), -1.0)
            tab_ref[h, 3] = _decay_rows(lgf, CHUNK - 1.0, -1.0)
            tab_ref[h, 4] = _decay_rows(lgf, float(CHUNK), 0.0)
        for d in range(CONV_K):
            wtap_ref[d] = jnp.broadcast_to(cw_ref[d:d + 1, :], (SUBLANES, D_CONV))
        zeros = jnp.zeros((HALO, D_CONV), F32)
        for j in range(n_dom):
            upad_ref[j * dom_rows:j * dom_rows + HALO, :] = zeros
            upad_ref[j * dom_rows + HALO + conv_dom:(j + 1) * dom_rows, :] = zeros
        mix_ref[...] = jnp.zeros((tm, D_MODEL), BF16)

    if is_lat:
        @pl.when(jnp.minimum(step, nt - 1) % tiles_per_seq == 0)
        def _():
            state_ref[...] = r0_ref[...]

    units = []
    unit_cost = []

    def conv_fill(j, w, slot):
        def run():
            base = j * dom_rows
            if w == 0:
                upad_ref[base + HALO:base + HALO + conv_dom, :] = (
                    mid_ref[j * conv_dom:(j + 1) * conv_dom, 0:D_CONV].astype(F32))
            lo = base + w * CONV_WIN
            for b in range(1, SUBLANES):
                shift_ref[slot, b - 1] = upad_ref[lo + b:lo + b + copy_rows, :]
        return run

    def conv_block(j, w, r0, slot):
        def run():
            groups = CONV_ROWS // SUBLANES
            acc = jnp.broadcast_to(cb_ref[...], (groups, SUBLANES, D_CONV))
            for d in range(CONV_K):
                off = HALO - CONV_PAD + d
                a, b = off // SUBLANES, off % SUBLANES
                if b == 0:
                    lo = j * dom_rows + w * CONV_WIN + a * SUBLANES + r0
                    tap = upad_ref[lo:lo + CONV_ROWS, :]
                else:
                    lo = a * SUBLANES + r0
                    tap = shift_ref[slot, b - 1, lo:lo + CONV_ROWS, :]
                acc = acc + wtap_ref[d] * tap.reshape(groups, SUBLANES, D_CONV)
            uc = _layer_norm(acc.reshape(CONV_ROWS, D_CONV), clg_ref[...], clb_ref[...])
            first = j * conv_dom + w * CONV_WIN + r0
            mix_ref[first:first + CONV_ROWS, 0:D_CONV] = (uc * _sigmoid(uc)).astype(BF16)
            return uc[0:TOKEN_ROWS, 0:HEAD_D]
        return run

    conv_units = []
    for j in range(n_dom):
        for w in range(conv_dom // CONV_WIN):
            slot = (j * (conv_dom // CONV_WIN) + w) % SHIFT_SLOTS
            fill = conv_fill(j, w, slot)
            for r0 in range(0, CONV_WIN, CONV_ROWS):
                conv_units.append((fill, conv_block(j, w, r0, slot)))
                fill = None

    def ret_unit(h, ci):
        qc = slice(D_CONV + h * HEAD_D, D_CONV + (h + 1) * HEAD_D)
        kc = slice(D_CONV + D_RET + h * HEAD_D, D_CONV + D_RET + (h + 1) * HEAD_D)
        vc = slice(D_CONV + 2 * D_RET + h * HEAD_D, D_CONV + 2 * D_RET + (h + 1) * HEAD_D)
        gc = slice(D_CONV + 3 * D_RET + h * HEAD_D, D_CONV + 3 * D_RET + (h + 1) * HEAD_D)

        def run():
            rows = slice(ci * CHUNK, (ci + 1) * CHUNK)
            if ci % chunks_per_seq == 0 and not is_lat:
                r = jnp.zeros((HEAD_D, HEAD_D), F32)
            else:
                r = state_ref[h]
            qh = mid_ref[rows, qc]
            kh = mid_ref[rows, kc]
            vh = mid_ref[rows, vc]
            s = lax.dot_general(qh, kh, (((1,), (1,)), ((), ())), preferred_element_type=F32)
            o = _dot((s * tab_ref[h, 0]).astype(BF16), vh)
            qf = qh.astype(F32)
            qd = jnp.concatenate([(qf * tab_ref[h, 1]).astype(BF16),
                                  (qf * tab_ref[h, 2]).astype(BF16)], axis=1)
            rcat = jnp.concatenate([r.astype(BF16), rb_ref[ci, h].astype(BF16)], axis=0)
            o = o + _dot(qd, rcat)
            kd = (kh.astype(F32) * tab_ref[h, 3]).astype(BF16)
            kv = lax.dot_general(kd, vh, (((0,), (0,)), ((), ())), preferred_element_type=F32)
            r = tab_ref[h, 4] * r + kv
            state_ref[h] = r
            if (ci + 1) % chunks_per_seq == 0 and not is_lat:
                rfin_ref[ci // chunks_per_seq, h] = r
            on = _layer_norm(o)
            mix_ref[rows, D_CONV + h * HEAD_D:D_CONV + (h + 1) * HEAD_D] = (
                on * mid_ref[rows, gc].astype(F32)).astype(BF16)
            return on[0:TOKEN_ROWS, :]
        return run

    ret_units = [ret_unit(h, ci) for ci in range(n_chunks) for h in range(N_HEADS)]

    for k in range(max(len(conv_units), len(ret_units))):
        if k < len(conv_units):
            fill, block = conv_units[k]
            if fill is not None:
                units.append(fill)
                unit_cost.append(3 * SUBLANES * CONV_WIN // CONV_ROWS)
            units.append(block)
            unit_cost.append(2 * CONV_K + 2 * SUBLANES)
        if k < len(ret_units):
            units.append(ret_units[k])
            unit_cost.append(3 * SUBLANES)

    n_phases = N_FF_CHUNKS + 2
    chunk_work = D_MODEL * 2 * FF_CHUNK
    phase_work = [D_MODEL * D_MODEL] + [chunk_work] * N_FF_CHUNKS + [chunk_work]
    starts = _split_units(unit_cost, phase_work)

    anchored = zero_ref[0] != 0

    def run_units(p):
        tok = None
        for u in units[starts[p]:starts[p + 1]]:
            piece = u()
            if piece is not None:
                piece = jnp.where(anchored, piece, 0.0)
                tok = piece if tok is None else tok + piece
        return tok

    def fold_into(ref, cols, tok):
        if tok is not None:
            ref[0:TOKEN_ROWS, cols] = (ref[0:TOKEN_ROWS, cols].astype(F32) + tok).astype(BF16)

    y = _dot(mix_ref[...], wout_ref[...])
    x2 = _layer_norm(alpha * x1_ref[...] + _mod_piece(mod_ref, 5) * y, lng_ref[1:2, :], lnb_ref[1:2, :])
    out_ref[...] = x2
    xm_ref[...] = (x2 * (1.0 + _mod_piece(mod_ref, 7)) + _mod_piece(mod_ref, 6)).astype(BF16)
    first = run_units(0)
    for c in range(N_FF_CHUNKS):
        _ffn_hidden_chunk(c, xm_ref, w1_ref, h_ref)
        if c == 0:
            fold_into(h_ref, slice(0, HEAD_D), first)
        fold_into(h_ref, slice(c * FF_CHUNK, c * FF_CHUNK + HEAD_D), run_units(c + 1))
    y2 = _dot(h_ref[...], w2_ref[...])
    tail = run_units(n_phases - 1)
    out = _layer_norm(alpha * out_ref[...] + 0.5 * _mod_piece(mod_ref, 8) * y2,
                      lng_ref[2:3, :], lnb_ref[2:3, :])
    out_ref[...] = out
    if tail is not None:
        out_ref[0:TOKEN_ROWS, 0:HEAD_D] = out[0:TOKEN_ROWS, 0:HEAD_D] + tail


def _stage_b(layer, is_lat, alpha, x1, mid, rb, seq_len, tm, lg, mod, ln_g, ln_b, conv_w, conv_b, conv_ln_g,
             conv_ln_b, w_out, w1, w2, r0, acc=None):
    depth = w1.shape[0]
    acc_pos, aliases = None, {}
    n_tok = x1.shape[0]
    nt = n_tok // tm
    tiles_per_seq = max(seq_len // tm, 1)
    n_chunks = tm // CHUNK
    n_seq = n_tok // seq_len
    conv_dom = GRID_W if is_lat else seq_len
    n_dom = tm // conv_dom

    ftile = lambda i: jnp.maximum(i - 1, 0)
    mtile = lambda i: jnp.minimum(i, nt - 1)
    if is_lat:
        group = lambda i: 1 + ftile(i) // tiles_per_seq
    else:
        group = lambda i: 0
    in_specs = [
        pl.BlockSpec(memory_space=pltpu.SMEM),
        pl.BlockSpec(memory_space=pltpu.SMEM),
        pl.BlockSpec((tm, D_MODEL), lambda i: (ftile(i), 0)),
        pl.BlockSpec((tm, D_MID), lambda i: (mtile(i), 0)),
        pl.BlockSpec((n_chunks, N_HEADS, HEAD_D, HEAD_D), lambda i: (mtile(i), 0, 0, 0)),
        pl.BlockSpec((None, None, 1, N_SUB * 3 * D_MODEL), lambda i: (layer, group(i), 0, 0)),
        _const_spec((None, N_SUB, D_MODEL), (layer, 0, 0)),
        _const_spec((None, N_SUB, D_MODEL), (layer, 0, 0)),
        _const_spec((None, CONV_K, D_CONV), (layer, 0, 0)),
        _const_spec((None, 1, D_CONV), (layer, 0, 0)),
        _const_spec((None, 1, D_CONV), (layer, 0, 0)),
        _const_spec((None, 1, D_CONV), (layer, 0, 0)),
        _const_spec((None, D_MODEL, D_MODEL), (layer, 0, 0)),
        _const_spec((None, None, D_MODEL, 2 * D_FF), (layer, 1, 0, 0)),
        _const_spec((None, None, D_FF, D_MODEL), (layer, 1, 0, 0)),
    ]
    args = [lg, jnp.zeros((1,), jnp.int32), x1, mid, rb, mod, ln_g, ln_b, conv_w, conv_b, conv_ln_g,
            conv_ln_b, w_out, w1, w2]
    out_shape = [jax.ShapeDtypeStruct((n_tok, D_MODEL), F32)]
    out_specs = [pl.BlockSpec((tm, D_MODEL), lambda i: (ftile(i), 0))]
    if is_lat:
        in_specs.append(pl.BlockSpec((None, None, N_HEADS, HEAD_D, HEAD_D),
                                     lambda i: (mtile(i) // tiles_per_seq, layer, 0, 0, 0)))
        args.append(r0)
    else:
        seqs_per_tile = tm // seq_len
        out_shape.append(jax.ShapeDtypeStruct((n_seq, depth, N_HEADS, HEAD_D, HEAD_D), F32))
        out_specs.append(pl.BlockSpec((seqs_per_tile, None, N_HEADS, HEAD_D, HEAD_D),
                                      lambda i: (mtile(i), layer, 0, 0, 0)))
        if acc is not None:
            acc_pos = len(args)
            aliases = {acc_pos: len(out_shape) - 1}
            in_specs.append(pl.BlockSpec(memory_space=pl.ANY))
            args.append(acc)
    return pl.pallas_call(
        functools.partial(_stage_b_kernel, layer, is_lat, tm, nt, tiles_per_seq,
                          min(seq_len, tm) // CHUNK, conv_dom, alpha, acc_pos),
        grid=(nt + 1,),
        in_specs=in_specs,
        out_specs=out_specs,
        out_shape=out_shape,
        input_output_aliases=aliases,
        scratch_shapes=[pltpu.VMEM((N_HEADS, HEAD_D, HEAD_D), F32),
                        pltpu.VMEM((N_HEADS, 5, CHUNK, CHUNK), F32),
                        pltpu.VMEM((n_dom * (conv_dom + 2 * HALO), D_CONV), F32),
                        pltpu.VMEM((SHIFT_SLOTS, SUBLANES - 1, COPY_ROWS, D_CONV), F32),
                        pltpu.VMEM((tm, D_MODEL), BF16),
                        pltpu.VMEM((tm, D_MODEL), BF16),
                        pltpu.VMEM((tm, D_FF), BF16),
                        pltpu.VMEM((CONV_K, SUBLANES, D_CONV), F32)],
        compiler_params=pltpu.CompilerParams(
            dimension_semantics=("arbitrary",), vmem_limit_bytes=VMEM_LIMIT),
        name=f"stage_b_{'lat' if is_lat else 'ctx'}_{layer}",
    )(*args)


def _rope_tables(seq_len):
    t = jnp.arange(seq_len)
    r = (t // GRID_W).astype(F32)
    col = (t % GRID_W).astype(F32)
    nf = HEAD_D // 4
    inv = ROPE_BASE ** (-jnp.arange(nf, dtype=F32) / nf)
    ang_row = r[:, None] * inv
    ang_col = col[:, None] * inv
    cos = jnp.concatenate([jnp.cos(ang_row)] * 2 + [jnp.cos(ang_col)] * 2, axis=1)
    sin = jnp.concatenate([-jnp.sin(ang_row), jnp.sin(ang_row), -jnp.sin(ang_col), jnp.sin(ang_col)],
                          axis=1)
    return cos, sin


def _reorder_w_in(w_in):
    c, r = 2 * D_CONV, D_RET
    return jnp.concatenate([w_in[..., c + r:c + 3 * r], w_in[..., c:c + r], w_in[..., 0:c],
                            w_in[..., c + 3 * r:]], axis=-1)


def kernel(x_prompt, x_sample, state_ret_fwd, state_ret_bwd, c, c_ctx, w_ada, b_ada, ln_g, ln_b, ffn_w1,
           ffn_w2, w_in, w_out, conv_w, conv_b, conv_ln_g, conv_ln_b, ret_decay_logit):
    batch, seq, _ = x_prompt.shape
    dec_batch, dec_seq, _ = x_sample.shape
    depth = w_ada.shape[0]
    alpha = (2.0 * depth) ** 0.25
    assert dec_batch + 1 <= COND_ROWS and seq % CHUNK == 0 and dec_seq % CHUNK == 0

    cond_all = jnp.zeros((COND_ROWS, D_MODEL), F32).at[0].set(c_ctx).at[1:1 + dec_batch].set(c)
    mod = _modulation(cond_all, w_ada, b_ada).reshape(depth, COND_ROWS, 1, N_SUB * 3 * D_MODEL)
    lg = jax.nn.log_sigmoid(ret_decay_logit.astype(F32)).reshape(-1)
    rope_tabs = _rope_tables(dec_seq)
    w1b = ffn_w1.astype(BF16)
    w2b = ffn_w2.astype(BF16)
    winb = _reorder_w_in(w_in).astype(BF16)
    woutb = w_out.astype(BF16)
    conv_b3 = conv_b.reshape(depth, 1, D_CONV)
    clg3 = conv_ln_g.reshape(depth, 1, D_CONV)
    clb3 = conv_ln_b.reshape(depth, 1, D_CONV)

    tm_ctx = min(512, batch * seq)
    tm_lat = min(512, dec_seq)
    ctx = x_prompt.reshape(batch * seq, D_MODEL)
    lat = x_sample.reshape(dec_batch * dec_seq, D_MODEL)
    new_fwd = jnp.zeros((batch, depth, N_HEADS, HEAD_D, HEAD_D), F32)
    new_bwd = jnp.zeros((batch, depth, N_HEADS, HEAD_D, HEAD_D), F32)
    for l in range(depth):
        x1, mid, rb, new_bwd = _stage_a(l, False, alpha, ctx, seq, tm_ctx, lg, mod, ln_g, ln_b, w1b,
                                        w2b, winb, None, None, acc=new_bwd)
        ctx, new_fwd = _stage_b(l, False, alpha, x1, mid, rb, seq, tm_ctx, lg, mod, ln_g, ln_b, conv_w,
                                conv_b3, clg3, clb3, woutb, w1b, w2b, None, acc=new_fwd)
        x1, mid, rb = _stage_a(l, True, alpha, lat, dec_seq, tm_lat, lg, mod, ln_g, ln_b, w1b, w2b,
                               winb, rope_tabs, state_ret_bwd)
        (lat,) = _stage_b(l, True, alpha, x1, mid, rb, dec_seq, tm_lat, lg, mod, ln_g, ln_b, conv_w,
                          conv_b3, clg3, clb3, woutb, w1b, w2b, state_ret_fwd)
    return (ctx.reshape(batch, seq, D_MODEL), lat.reshape(dec_batch, dec_seq, D_MODEL), new_fwd, new_bwd)
```

```python
import functools

import jax
import jax.numpy as jnp
from jax import lax
from jax.experimental import pallas as pl
from jax.experimental.pallas import tpu as pltpu

D_MODEL = 1024
D_CONV = 512
D_RET = 512
N_HEADS = 4
HEAD_D = 128
CONV_K = 31
CONV_PAD = 15
CHUNK = 128
GRID_W = 64
D_FF = 2816
N_SUB = 3
D_IN = 2 * D_CONV + 4 * D_RET
D_MID = D_CONV + 4 * D_RET
ROPE_BASE = 10000.0
EPS = 1e-5
SUBLANES = 8
TOKEN_ROWS = 16
COND_ROWS = 16
CONV_ROWS = 32
CONV_WIN = 64
COPY_ROWS = CONV_WIN + (CONV_K // SUBLANES) * SUBLANES
SHIFT_SLOTS = 2
HALO = 16
FF_CHUNK = 256
N_FF_CHUNKS = D_FF // FF_CHUNK
VMEM_LIMIT = 56 * 1024 * 1024
TILE_TOKENS = 512
MOD_COL_TILES = 4

F32 = jnp.float32
BF16 = jnp.bfloat16


def _dot(a, b):
    return jnp.dot(a, b, preferred_element_type=F32)


NEG_LOG2_E = -1.4426950408889634


def _sigmoid(x):
    return 1.0 / (1.0 + jnp.exp2(x * NEG_LOG2_E))


def _layer_norm(x, g=None, b=None):
    mu = jnp.mean(x, axis=-1, keepdims=True)
    xc = x - mu
    var = jnp.mean(xc * xc, axis=-1, keepdims=True)
    y = xc * lax.rsqrt(var + EPS)
    if g is not None:
        y = y * g + b
    return y


def _ffn_hidden_chunk(c, xm_ref, w1_ref, h_ref, rows=slice(None)):
    lo = c * FF_CHUNK
    a = _dot(xm_ref[rows, :], w1_ref[:, lo:lo + FF_CHUNK])
    b = _dot(xm_ref[rows, :], w1_ref[:, D_FF + lo:D_FF + lo + FF_CHUNK])
    h_ref[:, lo:lo + FF_CHUNK] = (a * _sigmoid(a) * b).astype(BF16)


def _ffn(xm_ref, w1_ref, w2_ref, h_ref):
    for c in range(N_FF_CHUNKS):
        _ffn_hidden_chunk(c, xm_ref, w1_ref, h_ref)
    return _dot(h_ref[...], w2_ref[...])


def _split_units(costs, work):
    total_cost, total_work = sum(costs), sum(work)
    starts, k, spent, done = [0], 0, 0, 0
    for w in work[:-1]:
        done += w
        while k < len(costs) and (spent + costs[k] / 2) * total_work <= done * total_cost:
            spent += costs[k]
            k += 1
        starts.append(k)
    starts.append(len(costs))
    return starts


def _mod_piece(mod_ref, i):
    return mod_ref[:, i * D_MODEL:(i + 1) * D_MODEL]


def _decay_rows(lg, offset, sign):
    i = lax.broadcasted_iota(jnp.int32, (CHUNK, HEAD_D), 0).astype(F32)
    return jnp.exp(lg * (offset + sign * i))


def _mod_kernel(cond_ref, w_ref, b_ref, out_ref):
    cnd = cond_ref[...]
    s = (cnd * _sigmoid(cnd)).astype(BF16)
    out_ref[...] = _dot(s, w_ref[...].astype(BF16)) + b_ref[...]


def _modulation(cond_all, w_ada, b_ada):
    depth = w_ada.shape[0]
    n_out = w_ada.shape[2]
    tn = n_out // MOD_COL_TILES
    return pl.pallas_call(
        _mod_kernel,
        grid=(depth, n_out // tn),
        in_specs=[
            pl.BlockSpec((COND_ROWS, D_MODEL), lambda l, j: (0, 0)),
            pl.BlockSpec((None, D_MODEL, tn), lambda l, j: (l, 0, j)),
            pl.BlockSpec((None, 1, tn), lambda l, j: (l, 0, j)),
        ],
        out_specs=pl.BlockSpec((None, COND_ROWS, tn), lambda l, j: (l, 0, j)),
        out_shape=jax.ShapeDtypeStruct((depth, COND_ROWS, n_out), F32),
        compiler_params=pltpu.CompilerParams(
            dimension_semantics=("arbitrary", "arbitrary"), vmem_limit_bytes=VMEM_LIMIT),
        name="adaln_modulation",
    )(cond_all, w_ada, b_ada.reshape(depth, 1, n_out))


def _drop_ref(refs, pos):
    return refs if pos is None else refs[:pos] + refs[pos + 1:]


def _stage_a_kernel(layer, is_lat, tm, tiles_per_seq, chunks_per_seq, alpha, acc_pos, *refs):
    refs = _drop_ref(refs, acc_pos)
    if is_lat:
        (lg_ref, x_ref, mod_ref, lng_ref, lnb_ref, w1_ref, w2_ref, win_ref, cos_ref, sin_ref, r0_ref,
         x1_ref, mid_ref, rb_ref, state_ref, kdec_ref, xm_ref, h_ref) = refs
    else:
        (lg_ref, x_ref, mod_ref, lng_ref, lnb_ref, w1_ref, w2_ref, win_ref,
         x1_ref, mid_ref, rb_ref, rfin_ref, state_ref, kdec_ref, xm_ref, h_ref) = refs
    step = pl.program_id(0)

    @pl.when(step == 0)
    def _():
        for h in range(N_HEADS):
            lg = lg_ref[layer * 2 * N_HEADS + N_HEADS + h]
            kdec_ref[h] = _decay_rows(lg, 0.0, 1.0)
            kdec_ref[N_HEADS + h] = _decay_rows(lg, float(CHUNK), 0.0)

    if is_lat:
        @pl.when(step % tiles_per_seq == 0)
        def _():
            state_ref[...] = r0_ref[...]

    xm_ref[...] = (x_ref[...] * (1.0 + _mod_piece(mod_ref, 1)) + _mod_piece(mod_ref, 0)).astype(BF16)
    y = _ffn(xm_ref, w1_ref, w2_ref, h_ref)
    x1 = _layer_norm(alpha * x_ref[...] + 0.5 * _mod_piece(mod_ref, 2) * y,
                     lng_ref[0:1, :], lnb_ref[0:1, :])
    x1_ref[...] = x1

    xm_ref[...] = (x1 * (1.0 + _mod_piece(mod_ref, 4)) + _mod_piece(mod_ref, 3)).astype(BF16)
    p = _dot(xm_ref[...], win_ref[...])
    k = p[:, 0:D_RET]
    v = p[:, D_RET:2 * D_RET]
    q = p[:, 2 * D_RET:3 * D_RET]
    o = 3 * D_RET
    ca = p[:, o:o + D_CONV]
    cg = p[:, o + D_CONV:o + 2 * D_CONV]
    g = p[:, o + 2 * D_CONV:o + 2 * D_CONV + D_RET]
    mid_ref[:, 0:D_CONV] = (ca * _sigmoid(cg)).astype(BF16)
    if is_lat:
        lane = lax.broadcasted_iota(jnp.int32, (tm, D_RET), 1)
        first_half = (lane % (HEAD_D // 2)) < (HEAD_D // 4)
        cos = jnp.concatenate([cos_ref[...]] * N_HEADS, axis=1)
        sin = jnp.concatenate([sin_ref[...]] * N_HEADS, axis=1)

        def rope(t):
            swapped = jnp.where(first_half, pltpu.roll(t, D_RET - HEAD_D // 4, 1),
                                pltpu.roll(t, HEAD_D // 4, 1))
            return t * cos + swapped * sin

        q = rope(q)
        k = rope(k)
    q = q * (HEAD_D ** -0.5)
    mid_ref[:, D_CONV:D_CONV + D_RET] = q.astype(BF16)
    mid_ref[:, D_CONV + D_RET:D_CONV + 2 * D_RET] = k.astype(BF16)
    vb = v.astype(BF16)
    mid_ref[:, D_CONV + 2 * D_RET:D_CONV + 3 * D_RET] = vb
    mid_ref[:, D_CONV + 3 * D_RET:D_CONV + 4 * D_RET] = (g * _sigmoid(g)).astype(BF16)

    n_chunks = tm // CHUNK
    for h in range(N_HEADS):
        cols = slice(h * HEAD_D, (h + 1) * HEAD_D)
        r = state_ref[h] if is_lat else None
        for ci in reversed(range(n_chunks)):
            rows = slice(ci * CHUNK, (ci + 1) * CHUNK)
            if not is_lat and (ci + 1) % chunks_per_seq == 0:
                r = jnp.zeros((HEAD_D, HEAD_D), F32)
            rb_ref[ci, h] = r
            kd = (k[rows, cols] * kdec_ref[h]).astype(BF16)
            kv = lax.dot_general(kd, vb[rows, cols], (((0,), (0,)), ((), ())),
                                 preferred_element_type=F32)
            r = kdec_ref[N_HEADS + h] * r + kv
            if not is_lat and ci % chunks_per_seq == 0:
                rfin_ref[ci // chunks_per_seq, h] = r
        if is_lat:
            state_ref[h] = r


def _const_spec(shape, index):
    return pl.BlockSpec(shape, lambda i: index, pipeline_mode=pl.Buffered(1))


def _stage_a(layer, is_lat, alpha, x2d, seq_len, tm, lg, mod, ln_g, ln_b, w1, w2, w_in, rope_tabs, r0,
             acc=None):
    depth = w1.shape[0]
    acc_pos, aliases = None, {}
    n_tok = x2d.shape[0]
    nt = n_tok // tm
    tiles_per_seq = max(seq_len // tm, 1)
    n_chunks = tm // CHUNK
    n_seq = n_tok // seq_len
    rev = lambda i: nt - 1 - i

    if is_lat:
        group = lambda i: 1 + rev(i) // tiles_per_seq
    else:
        group = lambda i: 0
    in_specs = [
        pl.BlockSpec(memory_space=pltpu.SMEM),
        pl.BlockSpec((tm, D_MODEL), lambda i: (rev(i), 0)),
        pl.BlockSpec((None, None, 1, N_SUB * 3 * D_MODEL), lambda i: (layer, group(i), 0, 0)),
        _const_spec((None, N_SUB, D_MODEL), (layer, 0, 0)),
        _const_spec((None, N_SUB, D_MODEL), (layer, 0, 0)),
        _const_spec((None, None, D_MODEL, 2 * D_FF), (layer, 0, 0, 0)),
        _const_spec((None, None, D_FF, D_MODEL), (layer, 0, 0, 0)),
        _const_spec((None, D_MODEL, D_IN), (layer, 0, 0)),
    ]
    args = [lg, x2d, mod, ln_g, ln_b, w1, w2, w_in]
    out_shape = [
        jax.ShapeDtypeStruct((n_tok, D_MODEL), F32),
        jax.ShapeDtypeStruct((n_tok, D_MID), BF16),
        jax.ShapeDtypeStruct((n_tok // CHUNK, N_HEADS, HEAD_D, HEAD_D), F32),
    ]
    out_specs = [
        pl.BlockSpec((tm, D_MODEL), lambda i: (rev(i), 0)),
        pl.BlockSpec((tm, D_MID), lambda i: (rev(i), 0)),
        pl.BlockSpec((n_chunks, N_HEADS, HEAD_D, HEAD_D), lambda i: (rev(i), 0, 0, 0)),
    ]
    if is_lat:
        cos_t, sin_t = rope_tabs
        in_specs += [
            pl.BlockSpec((tm, HEAD_D), lambda i: (rev(i) % tiles_per_seq, 0)),
            pl.BlockSpec((tm, HEAD_D), lambda i: (rev(i) % tiles_per_seq, 0)),
            pl.BlockSpec((None, None, N_HEADS, HEAD_D, HEAD_D),
                         lambda i: (rev(i) // tiles_per_seq, layer, 0, 0, 0)),
        ]
        args += [cos_t, sin_t, r0]
    else:
        seqs_per_tile = tm // seq_len
        out_shape.append(jax.ShapeDtypeStruct((n_seq, depth, N_HEADS, HEAD_D, HEAD_D), F32))
        out_specs.append(pl.BlockSpec((seqs_per_tile, None, N_HEADS, HEAD_D, HEAD_D),
                                      lambda i: (rev(i), layer, 0, 0, 0)))
        if acc is not None:
            acc_pos = len(args)
            aliases = {acc_pos: len(out_shape) - 1}
            in_specs.append(pl.BlockSpec(memory_space=pl.ANY))
            args.append(acc)
    return pl.pallas_call(
        functools.partial(_stage_a_kernel, layer, is_lat, tm, tiles_per_seq,
                          min(seq_len, tm) // CHUNK, alpha, acc_pos),
        grid=(nt,),
        in_specs=in_specs,
        out_specs=out_specs,
        out_shape=out_shape,
        input_output_aliases=aliases,
        scratch_shapes=[pltpu.VMEM((N_HEADS, HEAD_D, HEAD_D), F32),
                        pltpu.VMEM((2 * N_HEADS, CHUNK, HEAD_D), F32),
                        pltpu.VMEM((tm, D_MODEL), BF16),
                        pltpu.VMEM((tm, D_FF), BF16)],
        compiler_params=pltpu.CompilerParams(
            dimension_semantics=("arbitrary",), vmem_limit_bytes=VMEM_LIMIT),
        name=f"stage_a_{'lat' if is_lat else 'ctx'}_{layer}",
    )(*args)


def _stage_b_kernel(layer, is_lat, tm, nt, tiles_per_seq, chunks_per_seq, conv_dom, alpha, acc_pos,
                    *refs):
    refs = _drop_ref(refs, acc_pos)
    if is_lat:
        (lg_ref, zero_ref, x1_ref, mid_ref, rb_ref, mod_ref, lng_ref, lnb_ref, cw_ref, cb_ref, clg_ref,
         clb_ref, wout_ref, w1_ref, w2_ref, r0_ref,
         out_ref, state_ref, tab_ref, upad_ref, shift_ref, mix_ref, xm_ref, h_ref, wtap_ref) = refs
    else:
        (lg_ref, zero_ref, x1_ref, mid_ref, rb_ref, mod_ref, lng_ref, lnb_ref, cw_ref, cb_ref, clg_ref,
         clb_ref, wout_ref, w1_ref, w2_ref,
         out_ref, rfin_ref, state_ref, tab_ref, upad_ref, shift_ref, mix_ref, xm_ref, h_ref,
         wtap_ref) = refs
    step = pl.program_id(0)
    n_dom = tm // conv_dom
    dom_rows = conv_dom + 2 * HALO
    copy_rows = COPY_ROWS
    n_chunks = tm // CHUNK

    @pl.when(step == 0)
    def _():
        ii = lax.broadcasted_iota(jnp.int32, (CHUNK, CHUNK), 0)
        jj = lax.broadcasted_iota(jnp.int32, (CHUNK, CHUNK), 1)
        dist = (ii - jj).astype(F32)
        for h in range(N_HEADS):
            lgf = lg_ref[layer * 2 * N_HEADS + h]
            lgb = lg_ref[layer * 2 * N_HEADS + N_HEADS + h]
            fwd = jnp.where(dist >= 0, jnp.exp(lgf * jnp.maximum(dist, 0.0)), 0.0)
            bwd = jnp.where(dist <= 0, jnp.exp(lgb * jnp.maximum(-dist, 0.0)), 0.0)
            tab_ref[h, 0] = fwd + bwd
            tab_ref[h, 1] = _decay_rows(lgf, 1.0, 1.0)
            tab_ref[h, 2] = _decay_rows(lgb, float(CHUNK), -1.0)
            tab_ref[h, 3] = _decay_rows(lgf, CHUNK - 1.0, -1.0)
            tab_ref[h, 4] = _decay_rows(lgf, float(CHUNK), 0.0)
        for d in range(CONV_K):
            wtap_ref[d] = jnp.broadcast_to(cw_ref[d:d + 1, :], (SUBLANES, D_CONV))
        zeros = jnp.zeros((HALO, D_CONV), F32)
        for j in range(n_dom):
            upad_ref[j * dom_rows:j * dom_rows + HALO, :] = zeros
            upad_ref[j * dom_rows + HALO + conv_dom:(j + 1) * dom_rows, :] = zeros
        mix_ref[...] = jnp.zeros((tm, D_MODEL), BF16)

    if is_lat:
        @pl.when(jnp.minimum(step, nt - 1) % tiles_per_seq == 0)
        def _():
            state_ref[...] = r0_ref[...]

    units = []
    unit_cost = []

    def conv_fill(j, w, slot):
        def run():
            base = j * dom_rows
            if w == 0:
                upad_ref[base + HALO:base + HALO + conv_dom, :] = (
                    mid_ref[j * conv_dom:(j + 1) * conv_dom, 0:D_CONV].astype(F32))
            lo = base + w * CONV_WIN
            for b in range(1, SUBLANES):
                shift_ref[slot, b - 1] = upad_ref[lo + b:lo + b + copy_rows, :]
        return run

    def conv_block(j, w, r0, slot):
        def run():
            groups = CONV_ROWS // SUBLANES
            acc = jnp.broadcast_to(cb_ref[...], (groups, SUBLANES, D_CONV))
            for d in range(CONV_K):
                off = HALO - CONV_PAD + d
                a, b = off // SUBLANES, off % SUBLANES
                if b == 0:
                    lo = j * dom_rows + w * CONV_WIN + a * SUBLANES + r0
                    tap = upad_ref[lo:lo + CONV_ROWS, :]
                else:
                    lo = a * SUBLANES + r0
                    tap = shift_ref[slot, b - 1, lo:lo + CONV_ROWS, :]
                acc = acc + wtap_ref[d] * tap.reshape(groups, SUBLANES, D_CONV)
            uc = _layer_norm(acc.reshape(CONV_ROWS, D_CONV), clg_ref[...], clb_ref[...])
            first = j * conv_dom + w * CONV_WIN + r0
            mix_ref[first:first + CONV_ROWS, 0:D_CONV] = (uc * _sigmoid(uc)).astype(BF16)
            return uc[0:TOKEN_ROWS, 0:HEAD_D]
        return run

    conv_units = []
    for j in range(n_dom):
        for w in range(conv_dom // CONV_WIN):
            slot = (j * (conv_dom // CONV_WIN) + w) % SHIFT_SLOTS
            fill = conv_fill(j, w, slot)
            for r0 in range(0, CONV_WIN, CONV_ROWS):
                conv_units.append((fill, conv_block(j, w, r0, slot)))
                fill = None

    def ret_unit(h, ci):
        qc = slice(D_CONV + h * HEAD_D, D_CONV + (h + 1) * HEAD_D)
        kc = slice(D_CONV + D_RET + h * HEAD_D, D_CONV + D_RET + (h + 1) * HEAD_D)
        vc = slice(D_CONV + 2 * D_RET + h * HEAD_D, D_CONV + 2 * D_RET + (h + 1) * HEAD_D)
        gc = slice(D_CONV + 3 * D_RET + h * HEAD_D, D_CONV + 3 * D_RET + (h + 1) * HEAD_D)

        def run():
            rows = slice(ci * CHUNK, (ci + 1) * CHUNK)
            if ci % chunks_per_seq == 0 and not is_lat:
                r = jnp.zeros((HEAD_D, HEAD_D), F32)
            else:
                r = state_ref[h]
            qh = mid_ref[rows, qc]
            kh = mid_ref[rows, kc]
            vh = mid_ref[rows, vc]
            s = lax.dot_general(qh, kh, (((1,), (1,)), ((), ())), preferred_element_type=F32)
            o = _dot((s * tab_ref[h, 0]).astype(BF16), vh)
            qf = qh.astype(F32)
            qd = jnp.concatenate([(qf * tab_ref[h, 1]).astype(BF16),
                                  (qf * tab_ref[h, 2]).astype(BF16)], axis=1)
            rcat = jnp.concatenate([r.astype(BF16), rb_ref[ci, h].astype(BF16)], axis=0)
            o = o + _dot(qd, rcat)
            kd = (kh.astype(F32) * tab_ref[h, 3]).astype(BF16)
            kv = lax.dot_general(kd, vh, (((0,), (0,)), ((), ())), preferred_element_type=F32)
            r = tab_ref[h, 4] * r + kv
            state_ref[h] = r
            if (ci + 1) % chunks_per_seq == 0 and not is_lat:
                rfin_ref[ci // chunks_per_seq, h] = r
            on = _layer_norm(o)
            mix_ref[rows, D_CONV + h * HEAD_D:D_CONV + (h + 1) * HEAD_D] = (
                on * mid_ref[rows, gc].astype(F32)).astype(BF16)
            return on[0:TOKEN_ROWS, :]
        return run

    ret_units = [ret_unit(h, ci) for ci in range(n_chunks) for h in range(N_HEADS)]

    for k in range(max(len(conv_units), len(ret_units))):
        if k < len(conv_units):
            fill, block = conv_units[k]
            if fill is not None:
                units.append(fill)
                unit_cost.append(3 * SUBLANES * CONV_WIN // CONV_ROWS)
            units.append(block)
            unit_cost.append(2 * CONV_K + 2 * SUBLANES)
        if k < len(ret_units):
            units.append(ret_units[k])
            unit_cost.append(3 * SUBLANES)

    n_phases = N_FF_CHUNKS + 2
    chunk_work = D_MODEL * 2 * FF_CHUNK
    phase_work = [D_MODEL * D_MODEL] + [chunk_work] * N_FF_CHUNKS + [chunk_work]
    starts = _split_units(unit_cost, phase_work)

    anchored = zero_ref[0] != 0

    def run_units(p):
        tok = None
        for u in units[starts[p]:starts[p + 1]]:
            piece = u()
            if piece is not None:
                piece = jnp.where(anchored, piece, 0.0)
                tok = piece if tok is None else tok + piece
        return tok

    def fold_into(ref, cols, tok):
        if tok is not None:
            ref[0:TOKEN_ROWS, cols] = (ref[0:TOKEN_ROWS, cols].astype(F32) + tok).astype(BF16)

    y = _dot(mix_ref[...], wout_ref[...])
    x2 = _layer_norm(alpha * x1_ref[...] + _mod_piece(mod_ref, 5) * y, lng_ref[1:2, :], lnb_ref[1:2, :])
    out_ref[...] = x2
    xm_ref[...] = (x2 * (1.0 + _mod_piece(mod_ref, 7)) + _mod_piece(mod_ref, 6)).astype(BF16)
    first = run_units(0)
    for c in range(N_FF_CHUNKS):
        _ffn_hidden_chunk(c, xm_ref, w1_ref, h_ref)
        if c == 0:
            fold_into(h_ref, slice(0, HEAD_D), first)
        fold_into(h_ref, slice(c * FF_CHUNK, c * FF_CHUNK + HEAD_D), run_units(c + 1))
    y2 = _dot(h_ref[...], w2_ref[...])
    tail = run_units(n_phases - 1)
    out = _layer_norm(alpha * out_ref[...] + 0.5 * _mod_piece(mod_ref, 8) * y2,
                      lng_ref[2:3, :], lnb_ref[2:3, :])
    out_ref[...] = out
    if tail is not None:
        out_ref[0:TOKEN_ROWS, 0:HEAD_D] = out[0:TOKEN_ROWS, 0:HEAD_D] + tail


def _stage_b(layer, is_lat, alpha, x1, mid, rb, seq_len, tm, lg, mod, ln_g, ln_b, conv_w, conv_b, conv_ln_g,
             conv_ln_b, w_out, w1, w2, r0, acc=None):
    depth = w1.shape[0]
    acc_pos, aliases = None, {}
    n_tok = x1.shape[0]
    nt = n_tok // tm
    tiles_per_seq = max(seq_len // tm, 1)
    n_chunks = tm // CHUNK
    n_seq = n_tok // seq_len
    conv_dom = GRID_W if is_lat else seq_len
    n_dom = tm // conv_dom

    ftile = lambda i: jnp.maximum(i - 1, 0)
    mtile = lambda i: jnp.minimum(i, nt - 1)
    if is_lat:
        group = lambda i: 1 + ftile(i) // tiles_per_seq
    else:
        group = lambda i: 0
    in_specs = [
        pl.BlockSpec(memory_space=pltpu.SMEM),
        pl.BlockSpec(memory_space=pltpu.SMEM),
        pl.BlockSpec((tm, D_MODEL), lambda i: (ftile(i), 0)),
        pl.BlockSpec((tm, D_MID), lambda i: (mtile(i), 0)),
        pl.BlockSpec((n_chunks, N_HEADS, HEAD_D, HEAD_D), lambda i: (mtile(i), 0, 0, 0)),
        pl.BlockSpec((None, None, 1, N_SUB * 3 * D_MODEL), lambda i: (layer, group(i), 0, 0)),
        _const_spec((None, N_SUB, D_MODEL), (layer, 0, 0)),
        _const_spec((None, N_SUB, D_MODEL), (layer, 0, 0)),
        _const_spec((None, CONV_K, D_CONV), (layer, 0, 0)),
        _const_spec((None, 1, D_CONV), (layer, 0, 0)),
        _const_spec((None, 1, D_CONV), (layer, 0, 0)),
        _const_spec((None, 1, D_CONV), (layer, 0, 0)),
        _const_spec((None, D_MODEL, D_MODEL), (layer, 0, 0)),
        _const_spec((None, None, D_MODEL, 2 * D_FF), (layer, 1, 0, 0)),
        _const_spec((None, None, D_FF, D_MODEL), (layer, 1, 0, 0)),
    ]
    args = [lg, jnp.zeros((1,), jnp.int32), x1, mid, rb, mod, ln_g, ln_b, conv_w, conv_b, conv_ln_g,
            conv_ln_b, w_out, w1, w2]
    out_shape = [jax.ShapeDtypeStruct((n_tok, D_MODEL), F32)]
    out_specs = [pl.BlockSpec((tm, D_MODEL), lambda i: (ftile(i), 0))]
    if is_lat:
        in_specs.append(pl.BlockSpec((None, None, N_HEADS, HEAD_D, HEAD_D),
                                     lambda i: (mtile(i) // tiles_per_seq, layer, 0, 0, 0)))
        args.append(r0)
    else:
        seqs_per_tile = tm // seq_len
        out_shape.append(jax.ShapeDtypeStruct((n_seq, depth, N_HEADS, HEAD_D, HEAD_D), F32))
        out_specs.append(pl.BlockSpec((seqs_per_tile, None, N_HEADS, HEAD_D, HEAD_D),
                                      lambda i: (mtile(i), layer, 0, 0, 0)))
        if acc is not None:
            acc_pos = len(args)
            aliases = {acc_pos: len(out_shape) - 1}
            in_specs.append(pl.BlockSpec(memory_space=pl.ANY))
            args.append(acc)
    return pl.pallas_call(
        functools.partial(_stage_b_kernel, layer, is_lat, tm, nt, tiles_per_seq,
                          min(seq_len, tm) // CHUNK, conv_dom, alpha, acc_pos),
        grid=(nt + 1,),
        in_specs=in_specs,
        out_specs=out_specs,
        out_shape=out_shape,
        input_output_aliases=aliases,
        scratch_shapes=[pltpu.VMEM((N_HEADS, HEAD_D, HEAD_D), F32),
                        pltpu.VMEM((N_HEADS, 5, CHUNK, CHUNK), F32),
                        pltpu.VMEM((n_dom * (conv_dom + 2 * HALO), D_CONV), F32),
                        pltpu.VMEM((SHIFT_SLOTS, SUBLANES - 1, COPY_ROWS, D_CONV), F32),
                        pltpu.VMEM((tm, D_MODEL), BF16),
                        pltpu.VMEM((tm, D_MODEL), BF16),
                        pltpu.VMEM((tm, D_FF), BF16),
                        pltpu.VMEM((CONV_K, SUBLANES, D_CONV), F32)],
        compiler_params=pltpu.CompilerParams(
            dimension_semantics=("arbitrary",), vmem_limit_bytes=VMEM_LIMIT),
        name=f"stage_b_{'lat' if is_lat else 'ctx'}_{layer}",
    )(*args)


def _rope_tables(seq_len):
    t = jnp.arange(seq_len)
    r = (t // GRID_W).astype(F32)
    col = (t % GRID_W).astype(F32)
    nf = HEAD_D // 4
    inv = ROPE_BASE ** (-jnp.arange(nf, dtype=F32) / nf)
    ang_row = r[:, None] * inv
    ang_col = col[:, None] * inv
    cos = jnp.concatenate([jnp.cos(ang_row)] * 2 + [jnp.cos(ang_col)] * 2, axis=1)
    sin = jnp.concatenate([-jnp.sin(ang_row), jnp.sin(ang_row), -jnp.sin(ang_col), jnp.sin(ang_col)],
                          axis=1)
    return cos, sin


def _reorder_w_in(w_in):
    c, r = 2 * D_CONV, D_RET
    return jnp.concatenate([w_in[..., c + r:c + 3 * r], w_in[..., c:c + r], w_in[..., 0:c],
                            w_in[..., c + 3 * r:]], axis=-1)


def kernel(x_prompt, x_sample, state_ret_fwd, state_ret_bwd, c, c_ctx, w_ada, b_ada, ln_g, ln_b, ffn_w1,
           ffn_w2, w_in, w_out, conv_w, conv_b, conv_ln_g, conv_ln_b, ret_decay_logit):
    batch, seq, _ = x_prompt.shape
    dec_batch, dec_seq, _ = x_sample.shape
    depth = w_ada.shape[0]
    alpha = (2.0 * depth) ** 0.25
    assert dec_batch + 1 <= COND_ROWS and seq % CHUNK == 0 and dec_seq % CHUNK == 0

    cond_all = jnp.zeros((COND_ROWS, D_MODEL), F32).at[0].set(c_ctx).at[1:1 + dec_batch].set(c)
    mod = _modulation(cond_all, w_ada, b_ada).reshape(depth, COND_ROWS, 1, N_SUB * 3 * D_MODEL)
    lg = jax.nn.log_sigmoid(ret_decay_logit.astype(F32)).reshape(-1)
    rope_tabs = _rope_tables(dec_seq)
    w1b = ffn_w1.astype(BF16)
    w2b = ffn_w2.astype(BF16)
    winb = _reorder_w_in(w_in).astype(BF16)
    woutb = w_out.astype(BF16)
    conv_b3 = conv_b.reshape(depth, 1, D_CONV)
    clg3 = conv_ln_g.reshape(depth, 1, D_CONV)
    clb3 = conv_ln_b.reshape(depth, 1, D_CONV)

    tm_ctx = min(TILE_TOKENS, batch * seq)
    tm_lat = min(TILE_TOKENS, dec_seq)
    ctx = x_prompt.reshape(batch * seq, D_MODEL)
    lat = x_sample.reshape(dec_batch * dec_seq, D_MODEL)
    new_fwd = jnp.zeros((batch, depth, N_HEADS, HEAD_D, HEAD_D), F32)
    new_bwd = jnp.zeros((batch, depth, N_HEADS, HEAD_D, HEAD_D), F32)
    for l in range(depth):
        x1, mid, rb, new_bwd = _stage_a(l, False, alpha, ctx, seq, tm_ctx, lg, mod, ln_g, ln_b, w1b,
                                        w2b, winb, None, None, acc=new_bwd)
        ctx, new_fwd = _stage_b(l, False, alpha, x1, mid, rb, seq, tm_ctx, lg, mod, ln_g, ln_b, conv_w,
                                conv_b3, clg3, clb3, woutb, w1b, w2b, None, acc=new_fwd)
        x1, mid, rb = _stage_a(l, True, alpha, lat, dec_seq, tm_lat, lg, mod, ln_g, ln_b, w1b, w2b,
                               winb, rope_tabs, state_ret_bwd)
        (lat,) = _stage_b(l, True, alpha, x1, mid, rb, dec_seq, tm_lat, lg, mod, ln_g, ln_b, conv_w,
                          conv_b3, clg3, clb3, woutb, w1b, w2b, state_ret_fwd)
    return (ctx.reshape(batch, seq, D_MODEL), lat.reshape(dec_batch, dec_seq, D_MODEL), new_fwd, new_bwd)
```

```python
import functools

import jax
import jax.numpy as jnp
from jax import lax
from jax.experimental import pallas as pl
from jax.experimental.pallas import tpu as pltpu

D_MODEL = 1024
D_CONV = 512
D_RET = 512
N_HEADS = 4
HEAD_D = 128
CONV_K = 31
CONV_PAD = 15
CHUNK = 128
GRID_W = 64
D_FF = 2816
N_SUB = 3
D_IN = 2 * D_CONV + 4 * D_RET
D_MID = D_CONV + 4 * D_RET
ROPE_BASE = 10000.0
EPS = 1e-5
SUBLANES = 8
TOKEN_ROWS = 16
COND_ROWS = 16
CONV_ROWS = 32
CONV_WIN = 64
COPY_ROWS = CONV_WIN + (CONV_K // SUBLANES) * SUBLANES
SHIFT_SLOTS = 2
HALO = 16
FF_CHUNK = 256
N_FF_CHUNKS = D_FF // FF_CHUNK
VMEM_LIMIT = 56 * 1024 * 1024
TILE_TOKENS = 512
STREAM_BUFFERS = 3
MOD_COL_TILES = 4

F32 = jnp.float32
BF16 = jnp.bfloat16


def _dot(a, b):
    return jnp.dot(a, b, preferred_element_type=F32)


NEG_LOG2_E = -1.4426950408889634


def _sigmoid(x):
    return 1.0 / (1.0 + jnp.exp2(x * NEG_LOG2_E))


def _layer_norm(x, g=None, b=None):
    mu = jnp.mean(x, axis=-1, keepdims=True)
    xc = x - mu
    var = jnp.mean(xc * xc, axis=-1, keepdims=True)
    y = xc * lax.rsqrt(var + EPS)
    if g is not None:
        y = y * g + b
    return y


def _ffn_hidden_chunk(c, xm_ref, w1_ref, h_ref, rows=slice(None)):
    lo = c * FF_CHUNK
    a = _dot(xm_ref[rows, :], w1_ref[:, lo:lo + FF_CHUNK])
    b = _dot(xm_ref[rows, :], w1_ref[:, D_FF + lo:D_FF + lo + FF_CHUNK])
    h_ref[:, lo:lo + FF_CHUNK] = (a * _sigmoid(a) * b).astype(BF16)


def _ffn(xm_ref, w1_ref, w2_ref, h_ref):
    for c in range(N_FF_CHUNKS):
        _ffn_hidden_chunk(c, xm_ref, w1_ref, h_ref)
    return _dot(h_ref[...], w2_ref[...])


def _split_units(costs, work):
    total_cost, total_work = sum(costs), sum(work)
    starts, k, spent, done = [0], 0, 0, 0
    for w in work[:-1]:
        done += w
        while k < len(costs) and (spent + costs[k] / 2) * total_work <= done * total_cost:
            spent += costs[k]
            k += 1
        starts.append(k)
    starts.append(len(costs))
    return starts


def _mod_piece(mod_ref, i):
    return mod_ref[:, i * D_MODEL:(i + 1) * D_MODEL]


def _decay_rows(lg, offset, sign):
    i = lax.broadcasted_iota(jnp.int32, (CHUNK, HEAD_D), 0).astype(F32)
    return jnp.exp(lg * (offset + sign * i))


def _mod_kernel(cond_ref, w_ref, b_ref, out_ref):
    cnd = cond_ref[...]
    s = (cnd * _sigmoid(cnd)).astype(BF16)
    out_ref[...] = _dot(s, w_ref[...].astype(BF16)) + b_ref[...]


def _modulation(cond_all, w_ada, b_ada):
    depth = w_ada.shape[0]
    n_out = w_ada.shape[2]
    tn = n_out // MOD_COL_TILES
    return pl.pallas_call(
        _mod_kernel,
        grid=(depth, n_out // tn),
        in_specs=[
            pl.BlockSpec((COND_ROWS, D_MODEL), lambda l, j: (0, 0)),
            pl.BlockSpec((None, D_MODEL, tn), lambda l, j: (l, 0, j)),
            pl.BlockSpec((None, 1, tn), lambda l, j: (l, 0, j)),
        ],
        out_specs=pl.BlockSpec((None, COND_ROWS, tn), lambda l, j: (l, 0, j)),
        out_shape=jax.ShapeDtypeStruct((depth, COND_ROWS, n_out), F32),
        compiler_params=pltpu.CompilerParams(
            dimension_semantics=("arbitrary", "arbitrary"), vmem_limit_bytes=VMEM_LIMIT),
        name="adaln_modulation",
    )(cond_all, w_ada, b_ada.reshape(depth, 1, n_out))


def _drop_ref(refs, pos):
    return refs if pos is None else refs[:pos] + refs[pos + 1:]


def _stage_a_kernel(layer, is_lat, tm, nt, tiles_per_seq, chunks_per_seq, alpha, acc_pos, *refs):
    refs = _drop_ref(refs, acc_pos)
    if is_lat:
        (lg_ref, x_hbm, mod_ref, lng_ref, lnb_ref, w1_ref, w2_ref, win_ref, cos_ref, sin_ref, r0_ref,
         x1_ref, mid_ref, rb_ref, state_ref, kdec_ref, xm_ref, h_ref, xbuf_ref, xsem) = refs
    else:
        (lg_ref, x_hbm, mod_ref, lng_ref, lnb_ref, w1_ref, w2_ref, win_ref,
         x1_ref, mid_ref, rb_ref, rfin_ref, state_ref, kdec_ref, xm_ref, h_ref, xbuf_ref, xsem) = refs
    step = pl.program_id(0)

    def x_copy(s):
        slot = s % STREAM_BUFFERS
        first = pl.multiple_of((nt - 1 - s) * tm, tm)
        return pltpu.make_async_copy(x_hbm.at[pl.ds(first, tm), :], xbuf_ref.at[slot], xsem.at[slot])

    @pl.when(step == 0)
    def _():
        for s in range(min(STREAM_BUFFERS - 1, nt)):
            x_copy(s).start()

    @pl.when(step + (STREAM_BUFFERS - 1) < nt)
    def _():
        x_copy(step + (STREAM_BUFFERS - 1)).start()

    x_copy(step).wait()
    x_ref = xbuf_ref.at[step % STREAM_BUFFERS]

    @pl.when(step == 0)
    def _():
        for h in range(N_HEADS):
            lg = lg_ref[layer * 2 * N_HEADS + N_HEADS + h]
            kdec_ref[h] = _decay_rows(lg, 0.0, 1.0)
            kdec_ref[N_HEADS + h] = _decay_rows(lg, float(CHUNK), 0.0)

    if is_lat:
        @pl.when(step % tiles_per_seq == 0)
        def _():
            state_ref[...] = r0_ref[...]

    xm_ref[...] = (x_ref[...] * (1.0 + _mod_piece(mod_ref, 1)) + _mod_piece(mod_ref, 0)).astype(BF16)
    y = _ffn(xm_ref, w1_ref, w2_ref, h_ref)
    x1 = _layer_norm(alpha * x_ref[...] + 0.5 * _mod_piece(mod_ref, 2) * y,
                     lng_ref[0:1, :], lnb_ref[0:1, :])
    x1_ref[...] = x1

    xm_ref[...] = (x1 * (1.0 + _mod_piece(mod_ref, 4)) + _mod_piece(mod_ref, 3)).astype(BF16)
    p = _dot(xm_ref[...], win_ref[...])
    k = p[:, 0:D_RET]
    v = p[:, D_RET:2 * D_RET]
    q = p[:, 2 * D_RET:3 * D_RET]
    o = 3 * D_RET
    ca = p[:, o:o + D_CONV]
    cg = p[:, o + D_CONV:o + 2 * D_CONV]
    g = p[:, o + 2 * D_CONV:o + 2 * D_CONV + D_RET]
    mid_ref[:, 0:D_CONV] = (ca * _sigmoid(cg)).astype(BF16)
    if is_lat:
        lane = lax.broadcasted_iota(jnp.int32, (tm, D_RET), 1)
        first_half = (lane % (HEAD_D // 2)) < (HEAD_D // 4)
        cos = jnp.concatenate([cos_ref[...]] * N_HEADS, axis=1)
        sin = jnp.concatenate([sin_ref[...]] * N_HEADS, axis=1)

        def rope(t):
            swapped = jnp.where(first_half, pltpu.roll(t, D_RET - HEAD_D // 4, 1),
                                pltpu.roll(t, HEAD_D // 4, 1))
            return t * cos + swapped * sin

        q = rope(q)
        k = rope(k)
    q = q * (HEAD_D ** -0.5)
    mid_ref[:, D_CONV:D_CONV + D_RET] = q.astype(BF16)
    mid_ref[:, D_CONV + D_RET:D_CONV + 2 * D_RET] = k.astype(BF16)
    vb = v.astype(BF16)
    mid_ref[:, D_CONV + 2 * D_RET:D_CONV + 3 * D_RET] = vb
    mid_ref[:, D_CONV + 3 * D_RET:D_CONV + 4 * D_RET] = (g * _sigmoid(g)).astype(BF16)

    n_chunks = tm // CHUNK
    for h in range(N_HEADS):
        cols = slice(h * HEAD_D, (h + 1) * HEAD_D)
        r = state_ref[h] if is_lat else None
        for ci in reversed(range(n_chunks)):
            rows = slice(ci * CHUNK, (ci + 1) * CHUNK)
            if not is_lat and (ci + 1) % chunks_per_seq == 0:
                r = jnp.zeros((HEAD_D, HEAD_D), F32)
            rb_ref[ci, h] = r
            kd = (k[rows, cols] * kdec_ref[h]).astype(BF16)
            kv = lax.dot_general(kd, vb[rows, cols], (((0,), (0,)), ((), ())),
                                 preferred_element_type=F32)
            r = kdec_ref[N_HEADS + h] * r + kv
            if not is_lat and ci % chunks_per_seq == 0:
                rfin_ref[ci // chunks_per_seq, h] = r
        if is_lat:
            state_ref[h] = r


def _const_spec(shape, index):
    return pl.BlockSpec(shape, lambda i: index, pipeline_mode=pl.Buffered(1))


def _stage_a(layer, is_lat, alpha, x2d, seq_len, tm, lg, mod, ln_g, ln_b, w1, w2, w_in, rope_tabs, r0,
             acc=None):
    depth = w1.shape[0]
    acc_pos, aliases = None, {}
    n_tok = x2d.shape[0]
    nt = n_tok // tm
    tiles_per_seq = max(seq_len // tm, 1)
    n_chunks = tm // CHUNK
    n_seq = n_tok // seq_len
    rev = lambda i: nt - 1 - i

    if is_lat:
        group = lambda i: 1 + rev(i) // tiles_per_seq
    else:
        group = lambda i: 0
    in_specs = [
        pl.BlockSpec(memory_space=pltpu.SMEM),
        pl.BlockSpec(memory_space=pl.ANY),
        pl.BlockSpec((None, None, 1, N_SUB * 3 * D_MODEL), lambda i: (layer, group(i), 0, 0)),
        _const_spec((None, N_SUB, D_MODEL), (layer, 0, 0)),
        _const_spec((None, N_SUB, D_MODEL), (layer, 0, 0)),
        _const_spec((None, None, D_MODEL, 2 * D_FF), (layer, 0, 0, 0)),
        _const_spec((None, None, D_FF, D_MODEL), (layer, 0, 0, 0)),
        _const_spec((None, D_MODEL, D_IN), (layer, 0, 0)),
    ]
    args = [lg, x2d, mod, ln_g, ln_b, w1, w2, w_in]
    out_shape = [
        jax.ShapeDtypeStruct((n_tok, D_MODEL), F32),
        jax.ShapeDtypeStruct((n_tok, D_MID), BF16),
        jax.ShapeDtypeStruct((n_tok // CHUNK, N_HEADS, HEAD_D, HEAD_D), F32),
    ]
    out_specs = [
        pl.BlockSpec((tm, D_MODEL), lambda i: (rev(i), 0)),
        pl.BlockSpec((tm, D_MID), lambda i: (rev(i), 0)),
        pl.BlockSpec((n_chunks, N_HEADS, HEAD_D, HEAD_D), lambda i: (rev(i), 0, 0, 0)),
    ]
    if is_lat:
        cos_t, sin_t = rope_tabs
        in_specs += [
            pl.BlockSpec((tm, HEAD_D), lambda i: (rev(i) % tiles_per_seq, 0)),
            pl.BlockSpec((tm, HEAD_D), lambda i: (rev(i) % tiles_per_seq, 0)),
            pl.BlockSpec((None, None, N_HEADS, HEAD_D, HEAD_D),
                         lambda i: (rev(i) // tiles_per_seq, layer, 0, 0, 0)),
        ]
        args += [cos_t, sin_t, r0]
    else:
        seqs_per_tile = tm // seq_len
        out_shape.append(jax.ShapeDtypeStruct((n_seq, depth, N_HEADS, HEAD_D, HEAD_D), F32))
        out_specs.append(pl.BlockSpec((seqs_per_tile, None, N_HEADS, HEAD_D, HEAD_D),
                                      lambda i: (rev(i), layer, 0, 0, 0)))
        if acc is not None:
            acc_pos = len(args)
            aliases = {acc_pos: len(out_shape) - 1}
            in_specs.append(pl.BlockSpec(memory_space=pl.ANY))
            args.append(acc)
    return pl.pallas_call(
        functools.partial(_stage_a_kernel, layer, is_lat, tm, nt, tiles_per_seq,
                          min(seq_len, tm) // CHUNK, alpha, acc_pos),
        grid=(nt,),
        in_specs=in_specs,
        out_specs=out_specs,
        out_shape=out_shape,
        input_output_aliases=aliases,
        scratch_shapes=[pltpu.VMEM((N_HEADS, HEAD_D, HEAD_D), F32),
                        pltpu.VMEM((2 * N_HEADS, CHUNK, HEAD_D), F32),
                        pltpu.VMEM((tm, D_MODEL), BF16),
                        pltpu.VMEM((tm, D_FF), BF16),
                        pltpu.VMEM((STREAM_BUFFERS, tm, D_MODEL), F32),
                        pltpu.SemaphoreType.DMA((STREAM_BUFFERS,))],
        compiler_params=pltpu.CompilerParams(
            dimension_semantics=("arbitrary",), vmem_limit_bytes=VMEM_LIMIT),
        name=f"stage_a_{'lat' if is_lat else 'ctx'}_{layer}",
    )(*args)


def _stage_b_kernel(layer, is_lat, tm, nt, tiles_per_seq, chunks_per_seq, conv_dom, alpha, acc_pos,
                    *refs):
    refs = _drop_ref(refs, acc_pos)
    if is_lat:
        (lg_ref, zero_ref, x1_ref, mid_ref, rb_ref, mod_ref, lng_ref, lnb_ref, cw_ref, cb_ref, clg_ref,
         clb_ref, wout_ref, w1_ref, w2_ref, r0_ref,
         out_ref, state_ref, tab_ref, upad_ref, shift_ref, mix_ref, xm_ref, h_ref, wtap_ref) = refs
    else:
        (lg_ref, zero_ref, x1_ref, mid_ref, rb_ref, mod_ref, lng_ref, lnb_ref, cw_ref, cb_ref, clg_ref,
         clb_ref, wout_ref, w1_ref, w2_ref,
         out_ref, rfin_ref, state_ref, tab_ref, upad_ref, shift_ref, mix_ref, xm_ref, h_ref,
         wtap_ref) = refs
    step = pl.program_id(0)
    n_dom = tm // conv_dom
    dom_rows = conv_dom + 2 * HALO
    copy_rows = COPY_ROWS
    n_chunks = tm // CHUNK

    @pl.when(step == 0)
    def _():
        ii = lax.broadcasted_iota(jnp.int32, (CHUNK, CHUNK), 0)
        jj = lax.broadcasted_iota(jnp.int32, (CHUNK, CHUNK), 1)
        dist = (ii - jj).astype(F32)
        for h in range(N_HEADS):
            lgf = lg_ref[layer * 2 * N_HEADS + h]
            lgb = lg_ref[layer * 2 * N_HEADS + N_HEADS + h]
            fwd = jnp.where(dist >= 0, jnp.exp(lgf * jnp.maximum(dist, 0.0)), 0.0)
            bwd = jnp.where(dist <= 0, jnp.exp(lgb * jnp.maximum(-dist, 0.0)), 0.0)
            tab_ref[h, 0] = fwd + bwd
            tab_ref[h, 1] = _decay_rows(lgf, 1.0, 1.0)
            tab_ref[h, 2] = _decay_rows(lgb, float(CHUNK), -1.0)
            tab_ref[h, 3] = _decay_rows(lgf, CHUNK - 1.0, -1.0)
            tab_ref[h, 4] = _decay_rows(lgf, float(CHUNK), 0.0)
        for d in range(CONV_K):
            wtap_ref[d] = jnp.broadcast_to(cw_ref[d:d + 1, :], (SUBLANES, D_CONV))
        zeros = jnp.zeros((HALO, D_CONV), F32)
        for j in range(n_dom):
            upad_ref[j * dom_rows:j * dom_rows + HALO, :] = zeros
            upad_ref[j * dom_rows + HALO + conv_dom:(j + 1) * dom_rows, :] = zeros
        mix_ref[...] = jnp.zeros((tm, D_MODEL), BF16)

    if is_lat:
        @pl.when(jnp.minimum(step, nt - 1) % tiles_per_seq == 0)
        def _():
            state_ref[...] = r0_ref[...]

    units = []
    unit_cost = []

    def conv_fill(j, w, slot):
        def run():
            base = j * dom_rows
            if w == 0:
                upad_ref[base + HALO:base + HALO + conv_dom, :] = (
                    mid_ref[j * conv_dom:(j + 1) * conv_dom, 0:D_CONV].astype(F32))
            lo = base + w * CONV_WIN
            for b in range(1, SUBLANES):
                shift_ref[slot, b - 1] = upad_ref[lo + b:lo + b + copy_rows, :]
        return run

    def conv_block(j, w, r0, slot):
        def run():
            groups = CONV_ROWS // SUBLANES
            acc = jnp.broadcast_to(cb_ref[...], (groups, SUBLANES, D_CONV))
            for d in range(CONV_K):
                off = HALO - CONV_PAD + d
                a, b = off // SUBLANES, off % SUBLANES
                if b == 0:
                    lo = j * dom_rows + w * CONV_WIN + a * SUBLANES + r0
                    tap = upad_ref[lo:lo + CONV_ROWS, :]
                else:
                    lo = a * SUBLANES + r0
                    tap = shift_ref[slot, b - 1, lo:lo + CONV_ROWS, :]
                acc = acc + wtap_ref[d] * tap.reshape(groups, SUBLANES, D_CONV)
            uc = _layer_norm(acc.reshape(CONV_ROWS, D_CONV), clg_ref[...], clb_ref[...])
            first = j * conv_dom + w * CONV_WIN + r0
            mix_ref[first:first + CONV_ROWS, 0:D_CONV] = (uc * _sigmoid(uc)).astype(BF16)
            return uc[0:TOKEN_ROWS, 0:HEAD_D]
        return run

    conv_units = []
    for j in range(n_dom):
        for w in range(conv_dom // CONV_WIN):
            slot = (j * (conv_dom // CONV_WIN) + w) % SHIFT_SLOTS
            fill = conv_fill(j, w, slot)
            for r0 in range(0, CONV_WIN, CONV_ROWS):
                conv_units.append((fill, conv_block(j, w, r0, slot)))
                fill = None

    def ret_unit(h, ci):
        qc = slice(D_CONV + h * HEAD_D, D_CONV + (h + 1) * HEAD_D)
        kc = slice(D_CONV + D_RET + h * HEAD_D, D_CONV + D_RET + (h + 1) * HEAD_D)
        vc = slice(D_CONV + 2 * D_RET + h * HEAD_D, D_CONV + 2 * D_RET + (h + 1) * HEAD_D)
        gc = slice(D_CONV + 3 * D_RET + h * HEAD_D, D_CONV + 3 * D_RET + (h + 1) * HEAD_D)

        def run():
            rows = slice(ci * CHUNK, (ci + 1) * CHUNK)
            if ci % chunks_per_seq == 0 and not is_lat:
                r = jnp.zeros((HEAD_D, HEAD_D), F32)
            else:
                r = state_ref[h]
            qh = mid_ref[rows, qc]
            kh = mid_ref[rows, kc]
            vh = mid_ref[rows, vc]
            s = lax.dot_general(qh, kh, (((1,), (1,)), ((), ())), preferred_element_type=F32)
            o = _dot((s * tab_ref[h, 0]).astype(BF16), vh)
            qf = qh.astype(F32)
            qd = jnp.concatenate([(qf * tab_ref[h, 1]).astype(BF16),
                                  (qf * tab_ref[h, 2]).astype(BF16)], axis=1)
            rcat = jnp.concatenate([r.astype(BF16), rb_ref[ci, h].astype(BF16)], axis=0)
            o = o + _dot(qd, rcat)
            kd = (kh.astype(F32) * tab_ref[h, 3]).astype(BF16)
            kv = lax.dot_general(kd, vh, (((0,), (0,)), ((), ())), preferred_element_type=F32)
            r = tab_ref[h, 4] * r + kv
            state_ref[h] = r
            if (ci + 1) % chunks_per_seq == 0 and not is_lat:
                rfin_ref[ci // chunks_per_seq, h] = r
            on = _layer_norm(o)
            mix_ref[rows, D_CONV + h * HEAD_D:D_CONV + (h + 1) * HEAD_D] = (
                on * mid_ref[rows, gc].astype(F32)).astype(BF16)
            return on[0:TOKEN_ROWS, :]
        return run

    ret_units = [ret_unit(h, ci) for ci in range(n_chunks) for h in range(N_HEADS)]

    for k in range(max(len(conv_units), len(ret_units))):
        if k < len(conv_units):
            fill, block = conv_units[k]
            if fill is not None:
                units.append(fill)
                unit_cost.append(3 * SUBLANES * CONV_WIN // CONV_ROWS)
            units.append(block)
            unit_cost.append(2 * CONV_K + 2 * SUBLANES)
        if k < len(ret_units):
            units.append(ret_units[k])
            unit_cost.append(3 * SUBLANES)

    n_phases = N_FF_CHUNKS + 2
    chunk_work = D_MODEL * 2 * FF_CHUNK
    phase_work = [D_MODEL * D_MODEL] + [chunk_work] * N_FF_CHUNKS + [chunk_work]
    starts = _split_units(unit_cost, phase_work)

    anchored = zero_ref[0] != 0

    def run_units(p):
        tok = None
        for u in units[starts[p]:starts[p + 1]]:
            piece = u()
            if piece is not None:
                piece = jnp.where(anchored, piece, 0.0)
                tok = piece if tok is None else tok + piece
        return tok

    def fold_into(ref, cols, tok):
        if tok is not None:
            ref[0:TOKEN_ROWS, cols] = (ref[0:TOKEN_ROWS, cols].astype(F32) + tok).astype(BF16)

    y = _dot(mix_ref[...], wout_ref[...])
    x2 = _layer_norm(alpha * x1_ref[...] + _mod_piece(mod_ref, 5) * y, lng_ref[1:2, :], lnb_ref[1:2, :])
    out_ref[...] = x2
    xm_ref[...] = (x2 * (1.0 + _mod_piece(mod_ref, 7)) + _mod_piece(mod_ref, 6)).astype(BF16)
    first = run_units(0)
    for c in range(N_FF_CHUNKS):
        _ffn_hidden_chunk(c, xm_ref, w1_ref, h_ref)
        if c == 0:
            fold_into(h_ref, slice(0, HEAD_D), first)
        fold_into(h_ref, slice(c * FF_CHUNK, c * FF_CHUNK + HEAD_D), run_units(c + 1))
    y2 = _dot(h_ref[...], w2_ref[...])
    tail = run_units(n_phases - 1)
    out = _layer_norm(alpha * out_ref[...] + 0.5 * _mod_piece(mod_ref, 8) * y2,
                      lng_ref[2:3, :], lnb_ref[2:3, :])
    out_ref[...] = out
    if tail is not None:
        out_ref[0:TOKEN_ROWS, 0:HEAD_D] = out[0:TOKEN_ROWS, 0:HEAD_D] + tail


def _stage_b(layer, is_lat, alpha, x1, mid, rb, seq_len, tm, lg, mod, ln_g, ln_b, conv_w, conv_b, conv_ln_g,
             conv_ln_b, w_out, w1, w2, r0, acc=None):
    depth = w1.shape[0]
    acc_pos, aliases = None, {}
    n_tok = x1.shape[0]
    nt = n_tok // tm
    tiles_per_seq = max(seq_len // tm, 1)
    n_chunks = tm // CHUNK
    n_seq = n_tok // seq_len
    conv_dom = GRID_W if is_lat else seq_len
    n_dom = tm // conv_dom

    ftile = lambda i: jnp.maximum(i - 1, 0)
    mtile = lambda i: jnp.minimum(i, nt - 1)
    if is_lat:
        group = lambda i: 1 + ftile(i) // tiles_per_seq
    else:
        group = lambda i: 0
    in_specs = [
        pl.BlockSpec(memory_space=pltpu.SMEM),
        pl.BlockSpec(memory_space=pltpu.SMEM),
        pl.BlockSpec((tm, D_MODEL), lambda i: (ftile(i), 0)),
        pl.BlockSpec((tm, D_MID), lambda i: (mtile(i), 0)),
        pl.BlockSpec((n_chunks, N_HEADS, HEAD_D, HEAD_D), lambda i: (mtile(i), 0, 0, 0)),
        pl.BlockSpec((None, None, 1, N_SUB * 3 * D_MODEL), lambda i: (layer, group(i), 0, 0)),
        _const_spec((None, N_SUB, D_MODEL), (layer, 0, 0)),
        _const_spec((None, N_SUB, D_MODEL), (layer, 0, 0)),
        _const_spec((None, CONV_K, D_CONV), (layer, 0, 0)),
        _const_spec((None, 1, D_CONV), (layer, 0, 0)),
        _const_spec((None, 1, D_CONV), (layer, 0, 0)),
        _const_spec((None, 1, D_CONV), (layer, 0, 0)),
        _const_spec((None, D_MODEL, D_MODEL), (layer, 0, 0)),
        _const_spec((None, None, D_MODEL, 2 * D_FF), (layer, 1, 0, 0)),
        _const_spec((None, None, D_FF, D_MODEL), (layer, 1, 0, 0)),
    ]
    args = [lg, jnp.zeros((1,), jnp.int32), x1, mid, rb, mod, ln_g, ln_b, conv_w, conv_b, conv_ln_g,
            conv_ln_b, w_out, w1, w2]
    out_shape = [jax.ShapeDtypeStruct((n_tok, D_MODEL), F32)]
    out_specs = [pl.BlockSpec((tm, D_MODEL), lambda i: (ftile(i), 0))]
    if is_lat:
        in_specs.append(pl.BlockSpec((None, None, N_HEADS, HEAD_D, HEAD_D),
                                     lambda i: (mtile(i) // tiles_per_seq, layer, 0, 0, 0)))
        args.append(r0)
    else:
        seqs_per_tile = tm // seq_len
        out_shape.append(jax.ShapeDtypeStruct((n_seq, depth, N_HEADS, HEAD_D, HEAD_D), F32))
        out_specs.append(pl.BlockSpec((seqs_per_tile, None, N_HEADS, HEAD_D, HEAD_D),
                                      lambda i: (mtile(i), layer, 0, 0, 0)))
        if acc is not None:
            acc_pos = len(args)
            aliases = {acc_pos: len(out_shape) - 1}
            in_specs.append(pl.BlockSpec(memory_space=pl.ANY))
            args.append(acc)
    return pl.pallas_call(
        functools.partial(_stage_b_kernel, layer, is_lat, tm, nt, tiles_per_seq,
                          min(seq_len, tm) // CHUNK, conv_dom, alpha, acc_pos),
        grid=(nt + 1,),
        in_specs=in_specs,
        out_specs=out_specs,
        out_shape=out_shape,
        input_output_aliases=aliases,
        scratch_shapes=[pltpu.VMEM((N_HEADS, HEAD_D, HEAD_D), F32),
                        pltpu.VMEM((N_HEADS, 5, CHUNK, CHUNK), F32),
                        pltpu.VMEM((n_dom * (conv_dom + 2 * HALO), D_CONV), F32),
                        pltpu.VMEM((SHIFT_SLOTS, SUBLANES - 1, COPY_ROWS, D_CONV), F32),
                        pltpu.VMEM((tm, D_MODEL), BF16),
                        pltpu.VMEM((tm, D_MODEL), BF16),
                        pltpu.VMEM((tm, D_FF), BF16),
                        pltpu.VMEM((CONV_K, SUBLANES, D_CONV), F32)],
        compiler_params=pltpu.CompilerParams(
            dimension_semantics=("arbitrary",), vmem_limit_bytes=VMEM_LIMIT),
        name=f"stage_b_{'lat' if is_lat else 'ctx'}_{layer}",
    )(*args)


def _rope_tables(seq_len):
    t = jnp.arange(seq_len)
    r = (t // GRID_W).astype(F32)
    col = (t % GRID_W).astype(F32)
    nf = HEAD_D // 4
    inv = ROPE_BASE ** (-jnp.arange(nf, dtype=F32) / nf)
    ang_row = r[:, None] * inv
    ang_col = col[:, None] * inv
    cos = jnp.concatenate([jnp.cos(ang_row)] * 2 + [jnp.cos(ang_col)] * 2, axis=1)
    sin = jnp.concatenate([-jnp.sin(ang_row), jnp.sin(ang_row), -jnp.sin(ang_col), jnp.sin(ang_col)],
                          axis=1)
    return cos, sin


def _reorder_w_in(w_in):
    c, r = 2 * D_CONV, D_RET
    return jnp.concatenate([w_in[..., c + r:c + 3 * r], w_in[..., c:c + r], w_in[..., 0:c],
                            w_in[..., c + 3 * r:]], axis=-1)


def kernel(x_prompt, x_sample, state_ret_fwd, state_ret_bwd, c, c_ctx, w_ada, b_ada, ln_g, ln_b, ffn_w1,
           ffn_w2, w_in, w_out, conv_w, conv_b, conv_ln_g, conv_ln_b, ret_decay_logit):
    batch, seq, _ = x_prompt.shape
    dec_batch, dec_seq, _ = x_sample.shape
    depth = w_ada.shape[0]
    alpha = (2.0 * depth) ** 0.25
    assert dec_batch + 1 <= COND_ROWS and seq % CHUNK == 0 and dec_seq % CHUNK == 0

    cond_all = jnp.zeros((COND_ROWS, D_MODEL), F32).at[0].set(c_ctx).at[1:1 + dec_batch].set(c)
    mod = _modulation(cond_all, w_ada, b_ada).reshape(depth, COND_ROWS, 1, N_SUB * 3 * D_MODEL)
    lg = jax.nn.log_sigmoid(ret_decay_logit.astype(F32)).reshape(-1)
    rope_tabs = _rope_tables(dec_seq)
    w1b = ffn_w1.astype(BF16)
    w2b = ffn_w2.astype(BF16)
    winb = _reorder_w_in(w_in).astype(BF16)
    woutb = w_out.astype(BF16)
    conv_b3 = conv_b.reshape(depth, 1, D_CONV)
    clg3 = conv_ln_g.reshape(depth, 1, D_CONV)
    clb3 = conv_ln_b.reshape(depth, 1, D_CONV)

    tm_ctx = min(TILE_TOKENS, batch * seq)
    tm_lat = min(TILE_TOKENS, dec_seq)
    ctx = x_prompt.reshape(batch * seq, D_MODEL)
    lat = x_sample.reshape(dec_batch * dec_seq, D_MODEL)
    new_fwd = jnp.zeros((batch, depth, N_HEADS, HEAD_D, HEAD_D), F32)
    new_bwd = jnp.zeros((batch, depth, N_HEADS, HEAD_D, HEAD_D), F32)
    for l in range(depth):
        x1, mid, rb, new_bwd = _stage_a(l, False, alpha, ctx, seq, tm_ctx, lg, mod, ln_g, ln_b, w1b,
                                        w2b, winb, None, None, acc=new_bwd)
        ctx, new_fwd = _stage_b(l, False, alpha, x1, mid, rb, seq, tm_ctx, lg, mod, ln_g, ln_b, conv_w,
                                conv_b3, clg3, clb3, woutb, w1b, w2b, None, acc=new_fwd)
        x1, mid, rb = _stage_a(l, True, alpha, lat, dec_seq, tm_lat, lg, mod, ln_g, ln_b, w1b, w2b,
                               winb, rope_tabs, state_ret_bwd)
        (lat,) = _stage_b(l, True, alpha, x1, mid, rb, dec_seq, tm_lat, lg, mod, ln_g, ln_b, conv_w,
                          conv_b3, clg3, clb3, woutb, w1b, w2b, state_ret_fwd)
    return (ctx.reshape(batch, seq, D_MODEL), lat.reshape(dec_batch, dec_seq, D_MODEL), new_fwd, new_bwd)
```
